```python
import jax, jax.numpy as jnp
from jax import lax
import numpy as np

D_MODEL = 1024
BATCH = 8
SEQ = 8192
DEPTH = 1
DEC_BATCH = 4
DEC_SEQ = 8192
PAST_LEN = 128

D_CONV = D_MODEL
CONV_WIDTH = 3
GLA_HEADS = 4
GLA_DK = D_MODEL // 2
GLA_DV = D_MODEL
HEAD_K = GLA_DK // GLA_HEADS
HEAD_V = GLA_DV // GLA_HEADS
GATE_RANK = 16
GATE_TEMP = 16.0
CHUNK = 64
EPS = 1e-6

SPLIT_SIZES = (D_CONV, D_CONV, D_CONV, D_CONV,
               GLA_DK, GLA_DK, GLA_DV, GLA_DV,
               GATE_RANK, GATE_RANK,
               D_MODEL, D_MODEL)
N_IN = sum(SPLIT_SIZES)

kernel_name = 'hybrid_conv_gla_bidir_encoder'


def _split_points():
    return [int(v) for v in np.cumsum(SPLIT_SIZES)[:-1]]


def rmsnorm(x, g):
    xf = x.astype(jnp.float32)
    y = xf * lax.rsqrt(jnp.mean(xf * xf, axis=-1, keepdims=True) + EPS)
    return (y * g.astype(jnp.float32)).astype(x.dtype)


def depthwise_conv(u, w, b):
    c = u.shape[-1]
    out = lax.conv_general_dilated(
        u, w.astype(u.dtype)[:, None, :], window_strides=(1,),
        padding=[((CONV_WIDTH - 1) // 2, CONV_WIDTH // 2)],
        dimension_numbers=('NWC', 'WIO', 'NWC'), feature_group_count=c)
    return out + b.astype(u.dtype)


def gla_direction(q, k, v, g, strict):
    b, h, l, dk = q.shape
    dv = v.shape[-1]
    n = l // CHUNK

    def split(t):
        return t.reshape(b, h, n, CHUNK, t.shape[-1]).transpose(2, 0, 1, 3, 4)

    qc, kc, vc, gc = split(q), split(k), split(v), split(g)
    Gc = jnp.cumsum(gc, axis=-2)
    idx = jnp.arange(CHUNK)
    mask = (idx[:, None] > idx[None, :]) if strict else (idx[:, None] >= idx[None, :])

    def step(S, inp):
        qi, ki, vi, Gi = inp
        diff = Gi[..., :, None, :] - Gi[..., None, :, :]
        decay = jnp.exp(jnp.where(mask[:, :, None], diff, -jnp.inf))
        A = jnp.einsum('bhtsd,bhsd->bhts', qi[..., :, None, :] * decay, ki)
        o = (jnp.einsum('bhts,bhsv->bhtv', A, vi)
             + jnp.einsum('bhtd,bhdv->bhtv', qi * jnp.exp(Gi), S))
        G_last = Gi[..., -1:, :]
        S_new = (jnp.exp(G_last[..., 0, :])[..., None] * S
                 + jnp.einsum('bhsd,bhsv->bhdv', ki * jnp.exp(G_last - Gi), vi))
        return S_new, o

    S0 = jnp.zeros((b, h, dk, dv), jnp.float32)
    _, oc = lax.scan(step, S0, (qc, kc, vc, Gc))
    return oc.transpose(1, 2, 0, 3, 4).reshape(b, h, l, dv)


def encoder_layer(x, c, w_ada, b_ada, norm_g, w_in, conv_w, conv_b,
                  w_gate_f, b_gate_f, w_gate_b, b_gate_b, gla_norm_g,
                  w_a_out, w_b_out, w_out):
    bsz, l, _ = x.shape
    mod = jax.nn.silu(c) @ w_ada + b_ada
    shift, scale, gate = jnp.split(mod, 3, axis=-1)
    h = rmsnorm(x, norm_g) * (1.0 + scale[:, None, :]) + shift[:, None, :]

    proj = h @ w_in
    (a_b, a_c, a_x, a_z, q, k, v, b_z, lr_f, lr_b, m_a, m_b) = jnp.split(proj, _split_points(), axis=-1)

    u = depthwise_conv(a_c * a_x, conv_w, conv_b)
    y_a = (a_b * u * jax.nn.silu(a_z)) @ w_a_out

    def heads(t, d):
        return t.reshape(bsz, l, GLA_HEADS, d).transpose(0, 2, 1, 3).astype(jnp.float32)

    qh = heads(q, HEAD_K) * (HEAD_K ** -0.5)
    kh = heads(k, HEAD_K)
    vh = heads(v, HEAD_V)
    g_f = heads(jax.nn.log_sigmoid((lr_f @ w_gate_f + b_gate_f).astype(jnp.float32)) / GATE_TEMP, HEAD_K)
    g_b = heads(jax.nn.log_sigmoid((lr_b @ w_gate_b + b_gate_b).astype(jnp.float32)) / GATE_TEMP, HEAD_K)
    flip = lambda t: jnp.flip(t, axis=2)
    o = (gla_direction(qh, kh, vh, g_f, False)
         + flip(gla_direction(flip(qh), flip(kh), flip(vh), flip(g_b), True)))
    o = o * lax.rsqrt(jnp.mean(o * o, axis=-1, keepdims=True) + EPS)
    o = o.transpose(0, 2, 1, 3).reshape(bsz, l, GLA_DV) * gla_norm_g.astype(jnp.float32)
    y_b = (o.astype(x.dtype) * jax.nn.silu(b_z)) @ w_b_out

    merged = jax.nn.sigmoid(m_a) * y_a + jax.nn.sigmoid(m_b) * y_b
    return x + gate[:, None, :] * (merged @ w_out)


def encoder(x, c, w_ada, b_ada, norm_g, w_in, conv_w, conv_b, w_gate_f, b_gate_f,
            w_gate_b, b_gate_b, gla_norm_g, w_a_out, w_b_out, w_out, final_norm_g):
    for i in range(DEPTH):
        x = encoder_layer(x, c, w_ada[i], b_ada[i], norm_g[i], w_in[i], conv_w[i], conv_b[i],
                          w_gate_f[i], b_gate_f[i], w_gate_b[i], b_gate_b[i], gla_norm_g[i],
                          w_a_out[i], w_b_out[i], w_out[i])
    return rmsnorm(x, final_norm_g)


def setup_inputs(seed: int = 0) -> dict:
    key = jax.random.key(seed)
    ks = jax.random.split(key, 24)
    f32 = jnp.float32
    nrm = lambda k, shape, s: jax.random.normal(k, shape, f32) * s
    return {
        'x_prompt': nrm(ks[0], (BATCH, SEQ, D_MODEL), 1.0),
        'x_sample': nrm(ks[1], (DEC_BATCH, DEC_SEQ, D_MODEL), 1.0),
        'c_prompt': nrm(ks[2], (BATCH, D_MODEL), 1.0),
        'c_sample': nrm(ks[3], (DEC_BATCH, D_MODEL), 1.0),
        'w_ada': nrm(ks[4], (DEPTH, D_MODEL, 3 * D_MODEL), 0.5 * D_MODEL ** -0.5),
        'b_ada': nrm(ks[5], (DEPTH, 3 * D_MODEL), 0.02),
        'norm_g': 1.0 + nrm(ks[6], (DEPTH, D_MODEL), 0.02),
        'w_in': nrm(ks[7], (DEPTH, D_MODEL, N_IN), D_MODEL ** -0.5),
        'conv_w': nrm(ks[8], (DEPTH, CONV_WIDTH, D_CONV), CONV_WIDTH ** -0.5),
        'conv_b': nrm(ks[9], (DEPTH, D_CONV), 0.02),
        'w_gate_f': nrm(ks[10], (DEPTH, GATE_RANK, GLA_DK), GATE_RANK ** -0.5),
        'b_gate_f': 1.0 + nrm(ks[11], (DEPTH, GLA_DK), 0.5),
        'w_gate_b': nrm(ks[12], (DEPTH, GATE_RANK, GLA_DK), GATE_RANK ** -0.5),
        'b_gate_b': 1.0 + nrm(ks[13], (DEPTH, GLA_DK), 0.5),
        'gla_norm_g': 1.0 + nrm(ks[14], (DEPTH, GLA_DV), 0.02),
        'w_a_out': nrm(ks[15], (DEPTH, D_CONV, D_MODEL), D_CONV ** -0.5),
        'w_b_out': nrm(ks[16], (DEPTH, GLA_DV, D_MODEL), GLA_DV ** -0.5),
        'w_out': nrm(ks[17], (DEPTH, D_MODEL, D_MODEL), D_MODEL ** -0.5),
        'final_norm_g': 1.0 + nrm(ks[18], (D_MODEL,), 0.02),
    }


def reference(x_prompt, x_sample, c_prompt, c_sample, w_ada, b_ada, norm_g, w_in, conv_w, conv_b,
              w_gate_f, b_gate_f, w_gate_b, b_gate_b, gla_norm_g, w_a_out, w_b_out, w_out,
              final_norm_g):
    y_prompt = encoder(x_prompt, c_prompt, w_ada, b_ada, norm_g, w_in, conv_w, conv_b,
                       w_gate_f, b_gate_f, w_gate_b, b_gate_b, gla_norm_g, w_a_out, w_b_out,
                       w_out, final_norm_g)
    y_sample = encoder(x_sample, c_sample, w_ada, b_ada, norm_g, w_in, conv_w, conv_b,
                       w_gate_f, b_gate_f, w_gate_b, b_gate_b, gla_norm_g, w_a_out, w_b_out,
                       w_out, final_norm_g)
    return (y_prompt, y_sample)
```

```python
import functools

import jax
import jax.numpy as jnp
from jax import lax
from jax.experimental import pallas as pl
from jax.experimental.pallas import tpu as pltpu

D = 1024
HEADS = 4
DK = 512
HK = DK // HEADS
HV = D // HEADS
RANK = 16
GATE_TEMP = 16.0
EPS = 1e-6
CHUNK = 128
TILE = 256
HALO = 8
EXP_CLAMP = 80.0
VMEM_LIMIT = 56 * 1024 * 1024

F32 = jnp.float32
BF16 = jnp.bfloat16


def _dot(a, b):
    return jnp.dot(a, b, preferred_element_type=F32)


def _dot_nt(a, b):
    return lax.dot_general(a, b, (((1,), (1,)), ((), ())), preferred_element_type=F32)


def _dot_tn(a, b):
    return lax.dot_general(a, b, (((0,), (0,)), ((), ())), preferred_element_type=F32)


def _sigmoid(x):
    return 1.0 / (1.0 + jnp.exp(-x))


def _rms_scale(x):
    return lax.rsqrt(jnp.mean(x * x, axis=-1, keepdims=True) + EPS)


def _const_spec(shape):
    zeros = (0,) * len(shape)
    return pl.BlockSpec(shape, lambda b, i: zeros, pipeline_mode=pl.Buffered(1))


def _mod_kernel(c_ref, w_ref, b_ref, o_ref):
    c = c_ref[...]
    s = c * _sigmoid(c)
    o_ref[...] = jnp.dot(s, w_ref[...], preferred_element_type=F32,
                         precision=lax.Precision.HIGHEST) + b_ref[...]


def _modulation(c, w_ada, b_ada):
    n = c.shape[0]
    return pl.pallas_call(
        _mod_kernel,
        out_shape=jax.ShapeDtypeStruct((n, 3 * D), F32),
        compiler_params=pltpu.CompilerParams(vmem_limit_bytes=VMEM_LIMIT),
        name="adaln_mod",
    )(c, w_ada, b_ada)


def _pre_kernel(x_ref, xp_ref, xn_ref, mod_ref, ng_ref, wa_ref, wq_ref, wlr_ref, wm_ref,
                cw_ref, cb_ref, wg_ref, bg_ref, wao_ref,
                qaf_ref, kaf_ref, qab_ref, kab_ref, v_ref, sb_ref, smb_ref, yam_ref,
                sbt_ref, cvec_ref, state_ref, *, n_tiles):
    i = pl.program_id(1)
    tile = n_tiles - 1 - i
    T = x_ref.shape[1]

    @pl.when(i == 0)
    def _():
        state_ref[...] = jnp.zeros_like(state_ref)

    shift = mod_ref[0, 0:1, :]
    scale1 = 1.0 + mod_ref[0, 1:2, :]
    ng = ng_ref[...]

    def normed(xf):
        return (xf * _rms_scale(xf) * ng) * scale1 + shift

    hb = normed(x_ref[0]).astype(BF16)
    hh = normed(jnp.concatenate([xp_ref[0], xn_ref[0]], axis=0)).astype(BF16)

    pa = _dot(hb, wa_ref[...])
    p = pa[:, D:2 * D] * pa[:, 2 * D:3 * D]
    ph = _dot(hh, wa_ref[:, D:3 * D])
    ph = ph[:, :D] * ph[:, D:]
    p_before = jnp.where(tile > 0, ph[HALO - 1:HALO, :], 0.0)
    p_after = jnp.where(tile < n_tiles - 1, ph[HALO:HALO + 1, :], 0.0)
    row = lax.broadcasted_iota(jnp.int32, (T, 1), 0)
    p_prev = jnp.where(row == 0, p_before, pltpu.roll(p, 1, axis=0))
    p_next = jnp.where(row == T - 1, p_after, pltpu.roll(p, T - 1, axis=0))
    u = cw_ref[0:1, :] * p_prev + cw_ref[1:2, :] * p + cw_ref[2:3, :] * p_next + cb_ref[...]
    az = pa[:, 3 * D:]
    ga = (pa[:, :D] * u * (az * _sigmoid(az))).astype(BF16)
    y_a = _dot(ga, wao_ref[...])
    m = _dot(hb, wm_ref[...])
    yam_ref[0] = (_sigmoid(m[:, :D]) * y_a).astype(BF16)
    smb_ref[0] = _sigmoid(m[:, D:]).astype(BF16)

    qk = _dot(hb, wq_ref[...])
    q = qk[:, :DK] * (HK ** -0.5)
    k = qk[:, DK:2 * DK]
    vb = qk[:, 2 * DK:2 * DK + D].astype(BF16)
    v_ref[0] = vb
    bz = qk[:, 2 * DK + D:]
    sb_ref[0] = (bz * _sigmoid(bz)).astype(BF16)

    lr = _dot(hb, wlr_ref[...]).astype(BF16)
    z = _dot(lr, wg_ref[...]) + bg_ref[...]
    g = (jnp.minimum(z, 0.0) - jnp.log1p(jnp.exp(-jnp.abs(z)))) * (1.0 / GATE_TEMP)

    r = lax.broadcasted_iota(jnp.int32, (CHUNK, CHUNK), 0)
    c = lax.broadcasted_iota(jnp.int32, (CHUNK, CHUNK), 1)
    tri_incl = (c <= r).astype(BF16)
    tri_strict = (c < r).astype(BF16)
    mid = CHUNK // 2

    for ci in reversed(range(T // CHUNK)):
        rows = slice(ci * CHUNK, (ci + 1) * CHUNK)
        gc = g[rows]
        hi = gc.astype(BF16)
        lo = (gc - hi.astype(F32)).astype(BF16)
        gf = _dot(tri_incl, hi[:, :DK]) + _dot(tri_incl, lo[:, :DK])
        gm = gf[mid - 1:mid, :]
        gl = gf[CHUNK - 1:CHUNK, :]
        qaf_ref[0, rows, :] = (q[rows] * jnp.exp(jnp.minimum(gf - gm, EXP_CLAMP))).astype(BF16)
        kaf_ref[0, rows, :] = (k[rows] * jnp.exp(jnp.minimum(gm - gf, EXP_CLAMP))).astype(BF16)
        eb = _dot(tri_strict, hi[:, DK:]) + _dot(tri_strict, lo[:, DK:])
        em = eb[mid:mid + 1, :]
        et = eb[CHUNK - 1:CHUNK, :] + gc[CHUNK - 1:CHUNK, DK:]
        qab_ref[0, rows, :] = (q[rows] * jnp.exp(jnp.minimum(em - eb, EXP_CLAMP))).astype(BF16)
        kab = k[rows] * jnp.exp(jnp.minimum(eb - em, EXP_CLAMP))
        kab_ref[0, rows, :] = kab.astype(BF16)
        cvec_ref[0, ci] = jnp.concatenate(
            [jnp.exp(gl), jnp.exp(gl - gm), jnp.exp(gm), jnp.zeros((5, DK), F32)], axis=0)

        mu_b = jnp.exp(et - em)
        gamma_b = jnp.exp(et)
        ksb = (kab * jnp.exp(em)).astype(BF16)
        for h in range(HEADS):
            kc = slice(h * HK, (h + 1) * HK)
            st = state_ref[h]
            sbt_ref[0, ci, h * HV:(h + 1) * HV, :] = (st * mu_b[:, kc]).astype(BF16)
            state_ref[h] = st * gamma_b[:, kc] + _dot_tn(vb[rows, h * HV:(h + 1) * HV], ksb[:, kc])


def _pre(x, mod3, ng, wa, wq, wlr, wm, cw, cb, wg, bg, wao):
    B, L, _ = x.shape
    T = TILE
    nt = L // T
    nc = T // CHUNK
    tok = lambda w: pl.BlockSpec((1, T, w), lambda b, i: (b, nt - 1 - i, 0))
    halo_blocks = T // HALO
    in_specs = [
        tok(D),
        pl.BlockSpec((1, HALO, D), lambda b, i: (b, jnp.maximum((nt - 1 - i) * halo_blocks - 1, 0), 0)),
        pl.BlockSpec((1, HALO, D), lambda b, i: (b, jnp.minimum((nt - i) * halo_blocks, L // HALO - 1), 0)),
        pl.BlockSpec((1, 3, D), lambda b, i: (b, 0, 0)),
        _const_spec((1, D)),
        _const_spec((D, 4 * D)),
        _const_spec((D, 3 * D)),
        _const_spec((D, 128)),
        _const_spec((D, 2 * D)),
        _const_spec((3, D)),
        _const_spec((1, D)),
        _const_spec((128, 2 * DK)),
        _const_spec((1, 2 * DK)),
        _const_spec((D, D)),
    ]
    out_shape = [
        jax.ShapeDtypeStruct((B, L, DK), BF16),
        jax.ShapeDtypeStruct((B, L, DK), BF16),
        jax.ShapeDtypeStruct((B, L, DK), BF16),
        jax.ShapeDtypeStruct((B, L, DK), BF16),
        jax.ShapeDtypeStruct((B, L, D), BF16),
        jax.ShapeDtypeStruct((B, L, D), BF16),
        jax.ShapeDtypeStruct((B, L, D), BF16),
        jax.ShapeDtypeStruct((B, L, D), BF16),
        jax.ShapeDtypeStruct((B, L // CHUNK, HEADS * HV, HK), BF16),
        jax.ShapeDtypeStruct((B, L // CHUNK, 8, DK), F32),
    ]
    out_specs = [tok(DK), tok(DK), tok(DK), tok(DK), tok(D), tok(D), tok(D), tok(D),
                 pl.BlockSpec((1, nc, HEADS * HV, HK), lambda b, i: (b, nt - 1 - i, 0, 0)),
                 pl.BlockSpec((1, nc, 8, DK), lambda b, i: (b, nt - 1 - i, 0, 0))]
    return pl.pallas_call(
        functools.partial(_pre_kernel, n_tiles=nt),
        grid=(B, nt),
        in_specs=in_specs,
        out_specs=out_specs,
        out_shape=out_shape,
        scratch_shapes=[pltpu.VMEM((HEADS, HV, HK), F32)],
        compiler_params=pltpu.CompilerParams(
            dimension_semantics=("arbitrary", "arbitrary"), vmem_limit_bytes=VMEM_LIMIT),
        name="encoder_pre",
    )(x, x, x, mod3, ng, wa, wq, wlr, wm, cw, cb, wg, bg, wao)


def _post_kernel(x_ref, qaf_ref, kaf_ref, qab_ref, kab_ref, v_ref, sb_ref, smb_ref, yam_ref,
                 sbt_ref, cvec_ref, mod_ref, gln_ref, wbo_ref, wo_ref, fng_ref,
                 o_ref, state_ref, yin_ref):
    i = pl.program_id(1)
    T = x_ref.shape[1]

    @pl.when(i == 0)
    def _():
        state_ref[...] = jnp.zeros_like(state_ref)

    r = lax.broadcasted_iota(jnp.int32, (CHUNK, CHUNK), 0)
    c = lax.broadcasted_iota(jnp.int32, (CHUNK, CHUNK), 1)
    causal = c <= r

    for ci in range(T // CHUNK):
        rows = slice(ci * CHUNK, (ci + 1) * CHUNK)
        cv = cvec_ref[0, ci]
        gamma_f, delta_f, mu_f = cv[0:1, :], cv[1:2, :], cv[2:3, :]
        for h in range(HEADS):
            kc = slice(h * HK, (h + 1) * HK)
            vc = slice(h * HV, (h + 1) * HV)
            qaf = qaf_ref[0, rows, kc]
            kaf = kaf_ref[0, rows, kc]
            qab = qab_ref[0, rows, kc]
            vh = v_ref[0, rows, vc]
            a = jnp.where(causal, _dot_nt(qaf, kaf), _dot_nt(qab, kab_ref[0, rows, kc])).astype(BF16)
            st = state_ref[h]
            o = (_dot(a, vh)
                 + _dot_nt(qaf, (st * mu_f[:, kc]).astype(BF16))
                 + _dot_nt(qab, sbt_ref[0, ci, vc, :]))
            ksf = (kaf.astype(F32) * delta_f[:, kc]).astype(BF16)
            state_ref[h] = st * gamma_f[:, kc] + _dot_tn(vh, ksf)
            o = o * _rms_scale(o) * gln_ref[:, vc]
            yin_ref[rows, vc] = (o * sb_ref[0, rows, vc].astype(F32)).astype(BF16)

    y_b = _dot(yin_ref[...], wbo_ref[...])
    merged = (yam_ref[0].astype(F32) + smb_ref[0].astype(F32) * y_b).astype(BF16)
    y = x_ref[0] + mod_ref[0, 2:3, :] * _dot(merged, wo_ref[...])
    o_ref[0] = y * _rms_scale(y) * fng_ref[...]


def _post(x, qaf, kaf, qab, kab, v, sb, smb, yam, sbt, cvec, mod3, gln, wbo, wo, fng):
    B, L, _ = x.shape
    T = TILE
    nt = L // T
    nc = T // CHUNK
    tok = lambda w: pl.BlockSpec((1, T, w), lambda b, i: (b, i, 0))
    in_specs = [tok(D), tok(DK), tok(DK), tok(DK), tok(DK), tok(D), tok(D), tok(D), tok(D),
                pl.BlockSpec((1, nc, HEADS * HV, HK), lambda b, i: (b, i, 0, 0)),
                pl.BlockSpec((1, nc, 8, DK), lambda b, i: (b, i, 0, 0)),
                pl.BlockSpec((1, 3, D), lambda b, i: (b, 0, 0)),
                _const_spec((1, D)), _const_spec((D, D)), _const_spec((D, D)), _const_spec((1, D))]
    return pl.pallas_call(
        _post_kernel,
        grid=(B, nt),
        in_specs=in_specs,
        out_specs=tok(D),
        out_shape=jax.ShapeDtypeStruct((B, L, D), F32),
        scratch_shapes=[pltpu.VMEM((HEADS, HV, HK), F32), pltpu.VMEM((T, D), BF16)],
        compiler_params=pltpu.CompilerParams(
            dimension_semantics=("arbitrary", "arbitrary"), vmem_limit_bytes=VMEM_LIMIT),
        name="encoder_post",
    )(x, qaf, kaf, qab, kab, v, sb, smb, yam, sbt, cvec, mod3, gln, wbo, wo, fng)


def kernel(x_prompt, x_sample, c_prompt, c_sample, w_ada, b_ada, norm_g, w_in, conv_w, conv_b,
           w_gate_f, b_gate_f, w_gate_b, b_gate_b, gla_norm_g, w_a_out, w_b_out, w_out,
           final_norm_g):
    assert w_ada.shape[0] == 1, "one layer"
    w = w_in[0]
    o_q = 4 * D
    o_lr = o_q + 2 * DK + 2 * D
    o_m = o_lr + 2 * RANK
    wa = w[:, :o_q].astype(BF16)
    wq = w[:, o_q:o_lr].astype(BF16)
    wlr = jnp.pad(w[:, o_lr:o_m], ((0, 0), (0, 128 - 2 * RANK))).astype(BF16)
    wm = w[:, o_m:].astype(BF16)
    wg = jnp.zeros((128, 2 * DK), F32)
    wg = wg.at[:RANK, :DK].set(w_gate_f[0]).at[RANK:2 * RANK, DK:].set(w_gate_b[0]).astype(BF16)
    bg = jnp.concatenate([b_gate_f[0], b_gate_b[0]])[None, :]
    ng = norm_g[0][None, :]
    cw = conv_w[0]
    cb = conv_b[0][None, :]
    gln = gla_norm_g[0][None, :]
    wao = w_a_out[0].astype(BF16)
    wbo = w_b_out[0].astype(BF16)
    wo = w_out[0].astype(BF16)
    fng = final_norm_g[None, :]

    nb = x_prompt.shape[0]
    mod = _modulation(jnp.concatenate([c_prompt, c_sample], axis=0), w_ada[0], b_ada[0][None, :])
    mod3 = mod.reshape(mod.shape[0], 3, D)

    def encode(x, m3):
        pre = _pre(x, m3, ng, wa, wq, wlr, wm, cw, cb, wg, bg, wao)
        return _post(x, *pre, m3, gln, wbo, wo, fng)

    return (encode(x_prompt, mod3[:nb]), encode(x_sample, mod3[nb:]))
```

```python
import functools

import jax
import jax.numpy as jnp
from jax import lax
from jax.experimental import pallas as pl
from jax.experimental.pallas import tpu as pltpu

D = 1024
HEADS = 4
DK = 512
HK = DK // HEADS
HV = D // HEADS
RANK = 16
GATE_TEMP = 16.0
EPS = 1e-6
CHUNK = 128
TILE = 512
HALO = 16
SLAB = 256
EXP_CLAMP = 80.0
VMEM_LIMIT = 58 * 1024 * 1024

F32 = jnp.float32
BF16 = jnp.bfloat16


def _dot(a, b):
    return jnp.dot(a, b, preferred_element_type=F32)


def _dot_nt(a, b):
    return lax.dot_general(a, b, (((1,), (1,)), ((), ())), preferred_element_type=F32)


def _dot_tn(a, b):
    return lax.dot_general(a, b, (((0,), (0,)), ((), ())), preferred_element_type=F32)


def _sigmoid(x):
    return 0.5 + 0.5 * jnp.tanh(0.5 * x)


def _silu(x):
    h = 0.5 * x
    return h + h * jnp.tanh(h)


def _rms_scale(x):
    return lax.rsqrt(jnp.mean(x * x, axis=-1, keepdims=True) + EPS)


def _const_spec(shape):
    zeros = (0,) * len(shape)
    return pl.BlockSpec(shape, lambda b, i: zeros, pipeline_mode=pl.Buffered(1))


def _mod_kernel(c_ref, w_ref, b_ref, o_ref):
    o_ref[...] = jnp.dot(_silu(c_ref[...]), w_ref[...], preferred_element_type=F32,
                         precision=lax.Precision.HIGHEST) + b_ref[...]


def _modulation(c, w_ada, b_ada):
    n = c.shape[0]
    return pl.pallas_call(
        _mod_kernel,
        out_shape=jax.ShapeDtypeStruct((n, 3 * D), F32),
        compiler_params=pltpu.CompilerParams(vmem_limit_bytes=VMEM_LIMIT),
        name="adaln_mod",
    )(c, w_ada, b_ada)


def _pre_kernel(x_ref, xp_ref, xn_ref, mod_ref, ng_ref, wcx_ref, wbz_ref, wq_ref, wlr_ref, wm_ref,
                cw_ref, cb_ref, wg_ref, bg_ref, wao_ref,
                qaf_ref, kaf_ref, qab_ref, kab_ref, v_ref, sb_ref, smb_ref, yam_ref,
                sbt_ref, cvec_ref, state_ref, ga_ref, *, n_tiles):
    i = pl.program_id(1)
    tile = n_tiles - 1 - i
    T = x_ref.shape[1]

    @pl.when(i == 0)
    def _():
        state_ref[...] = jnp.zeros_like(state_ref)

    shift = mod_ref[0, 0:1, :]
    scale1 = 1.0 + mod_ref[0, 1:2, :]
    ng = ng_ref[...]

    def normed(xf):
        return ((xf * _rms_scale(xf) * ng) * scale1 + shift).astype(BF16)

    hb = normed(x_ref[0])
    hx = jnp.concatenate([normed(xp_ref[0]), hb, normed(xn_ref[0])], axis=0)

    lr = _dot(hb, wlr_ref[...]).astype(BF16)
    z = _dot(lr, wg_ref[...]) + bg_ref[...]
    g = (jnp.minimum(z, 0.0) - jnp.log1p(jnp.exp(-jnp.abs(z)))) * (1.0 / GATE_TEMP)

    smb_ref[0] = _sigmoid(_dot(hb, wm_ref[:, D:])).astype(BF16)

    qk = _dot(hb, wq_ref[...])
    q = qk[:, :DK] * (HK ** -0.5)
    k = qk[:, DK:2 * DK]
    vb = qk[:, 2 * DK:2 * DK + D].astype(BF16)
    v_ref[0] = vb
    sb_ref[0] = _silu(qk[:, 2 * DK + D:]).astype(BF16)

    r = lax.broadcasted_iota(jnp.int32, (CHUNK, 2 * CHUNK), 0)
    c = lax.broadcasted_iota(jnp.int32, (CHUNK, 2 * CHUNK), 1) % CHUNK
    tri_incl = (c <= r).astype(BF16)
    tri_strict = (c < r).astype(BF16)
    mid = CHUNK // 2

    for ci in reversed(range(T // CHUNK)):
        rows = slice(ci * CHUNK, (ci + 1) * CHUNK)
        gc = g[rows]
        hi = gc.astype(BF16)
        lo = (gc - hi.astype(F32)).astype(BF16)
        hilo = jnp.concatenate([hi, lo], axis=0)
        gf = _dot(tri_incl, hilo[:, :DK])
        gm = gf[mid - 1:mid, :]
        gl = gf[CHUNK - 1:CHUNK, :]
        qaf_ref[0, rows, :] = (q[rows] * jnp.exp(jnp.minimum(gf - gm, EXP_CLAMP))).astype(BF16)
        kaf_ref[0, rows, :] = (k[rows] * jnp.exp(jnp.minimum(gm - gf, EXP_CLAMP))).astype(BF16)
        eb = _dot(tri_strict, hilo[:, DK:])
        em = eb[mid:mid + 1, :]
        et = eb[CHUNK - 1:CHUNK, :] + gc[CHUNK - 1:CHUNK, DK:]
        qab_ref[0, rows, :] = (q[rows] * jnp.exp(jnp.minimum(em - eb, EXP_CLAMP))).astype(BF16)
        kab = k[rows] * jnp.exp(jnp.minimum(eb - em, EXP_CLAMP))
        kab_ref[0, rows, :] = kab.astype(BF16)
        cvec_ref[0, ci] = jnp.concatenate(
            [jnp.exp(gl), jnp.exp(gl - gm), jnp.exp(gm), jnp.zeros((5, DK), F32)], axis=0)

        mu_b = jnp.exp(et - em)
        gamma_b = jnp.exp(et)
        ksb = (kab * jnp.exp(em)).astype(BF16)
        for h in range(HEADS):
            kc = slice(h * HK, (h + 1) * HK)
            st = state_ref[h]
            sbt_ref[0, ci, h * HV:(h + 1) * HV, :] = (st * mu_b[:, kc]).astype(BF16)
            state_ref[h] = st * gamma_b[:, kc] + _dot_tn(vb[rows, h * HV:(h + 1) * HV], ksb[:, kc])

    rowx = lax.broadcasted_iota(jnp.int32, (T + 2 * HALO, 1), 0)
    first_valid = jnp.where(tile > 0, 0, HALO)
    end_valid = jnp.where(tile < n_tiles - 1, T + 2 * HALO, T + HALO)
    valid = (rowx >= first_valid) & (rowx < end_valid)
    inner = slice(HALO, T + HALO)
    for s in range(D // SLAB):
        cols = slice(s * SLAB, (s + 1) * SLAB)
        wcols = slice(2 * s * SLAB, 2 * (s + 1) * SLAB)
        pcx = _dot(hx, wcx_ref[:, wcols])
        pbz = _dot(hb, wbz_ref[:, wcols])
        p = jnp.where(valid, pcx[:, :SLAB] * pcx[:, SLAB:], 0.0)
        p_prev = pltpu.roll(p, 1, axis=0)[inner]
        p_next = pltpu.roll(p, T + 2 * HALO - 1, axis=0)[inner]
        u = (cw_ref[0:1, cols] * p_prev + cw_ref[1:2, cols] * p[inner] + cw_ref[2:3, cols] * p_next
             + cb_ref[:, cols])
        ga_ref[:, cols] = (pbz[:, :SLAB] * u * _silu(pbz[:, SLAB:])).astype(BF16)

    y_a = _dot(ga_ref[...], wao_ref[...])
    yam_ref[0] = (_sigmoid(_dot(hb, wm_ref[:, :D])) * y_a).astype(BF16)


def _pre(x, mod3, ng, wcx, wbz, wq, wlr, wm, cw, cb, wg, bg, wao):
    B, L, _ = x.shape
    T = TILE
    nt = L // T
    nc = T // CHUNK
    tok = lambda w: pl.BlockSpec((1, T, w), lambda b, i: (b, nt - 1 - i, 0))
    hpt = T // HALO
    in_specs = [
        tok(D),
        pl.BlockSpec((1, HALO, D), lambda b, i: (b, jnp.maximum((nt - 1 - i) * hpt - 1, 0), 0)),
        pl.BlockSpec((1, HALO, D), lambda b, i: (b, jnp.minimum((nt - i) * hpt, L // HALO - 1), 0)),
        pl.BlockSpec((1, 3, D), lambda b, i: (b, 0, 0)),
        _const_spec((1, D)),
        _const_spec((D, 2 * D)),
        _const_spec((D, 2 * D)),
        _const_spec((D, 3 * D)),
        _const_spec((D, 128)),
        _const_spec((D, 2 * D)),
        _const_spec((3, D)),
        _const_spec((1, D)),
        _const_spec((128, 2 * DK)),
        _const_spec((1, 2 * DK)),
        _const_spec((D, D)),
    ]
    out_shape = [
        jax.ShapeDtypeStruct((B, L, DK), BF16),
        jax.ShapeDtypeStruct((B, L, DK), BF16),
        jax.ShapeDtypeStruct((B, L, DK), BF16),
        jax.ShapeDtypeStruct((B, L, DK), BF16),
        jax.ShapeDtypeStruct((B, L, D), BF16),
        jax.ShapeDtypeStruct((B, L, D), BF16),
        jax.ShapeDtypeStruct((B, L, D), BF16),
        jax.ShapeDtypeStruct((B, L, D), BF16),
        jax.ShapeDtypeStruct((B, L // CHUNK, HEADS * HV, HK), BF16),
        jax.ShapeDtypeStruct((B, L // CHUNK, 8, DK), F32),
    ]
    out_specs = [tok(DK), tok(DK), tok(DK), tok(DK), tok(D), tok(D), tok(D), tok(D),
                 pl.BlockSpec((1, nc, HEADS * HV, HK), lambda b, i: (b, nt - 1 - i, 0, 0)),
                 pl.BlockSpec((1, nc, 8, DK), lambda b, i: (b, nt - 1 - i, 0, 0))]
    return pl.pallas_call(
        functools.partial(_pre_kernel, n_tiles=nt),
        grid=(B, nt),
        in_specs=in_specs,
        out_specs=out_specs,
        out_shape=out_shape,
        scratch_shapes=[pltpu.VMEM((HEADS, HV, HK), F32), pltpu.VMEM((T, D), BF16)],
        compiler_params=pltpu.CompilerParams(
            dimension_semantics=("arbitrary", "arbitrary"), vmem_limit_bytes=VMEM_LIMIT),
        name="encoder_pre",
    )(x, x, x, mod3, ng, wcx, wbz, wq, wlr, wm, cw, cb, wg, bg, wao)


def _post_kernel(x_ref, qaf_ref, kaf_ref, qab_ref, kab_ref, v_ref, sb_ref, smb_ref, yam_ref,
                 sbt_ref, cvec_ref, mod_ref, gln_ref, wbo_ref, wo_ref, fng_ref,
                 o_ref, state_ref, yin_ref):
    i = pl.program_id(1)
    T = x_ref.shape[1]

    @pl.when(i == 0)
    def _():
        state_ref[...] = jnp.zeros_like(state_ref)

    r = lax.broadcasted_iota(jnp.int32, (CHUNK, CHUNK), 0)
    c = lax.broadcasted_iota(jnp.int32, (CHUNK, CHUNK), 1)
    causal = c <= r

    for ci in range(T // CHUNK):
        rows = slice(ci * CHUNK, (ci + 1) * CHUNK)
        cv = cvec_ref[0, ci]
        gamma_f, delta_f, mu_f = cv[0:1, :], cv[1:2, :], cv[2:3, :]
        for h in range(HEADS):
            kc = slice(h * HK, (h + 1) * HK)
            vc = slice(h * HV, (h + 1) * HV)
            qaf = qaf_ref[0, rows, kc]
            kaf = kaf_ref[0, rows, kc]
            qab = qab_ref[0, rows, kc]
            vh = v_ref[0, rows, vc]
            a = jnp.where(causal, _dot_nt(qaf, kaf), _dot_nt(qab, kab_ref[0, rows, kc])).astype(BF16)
            st = state_ref[h]
            o = (_dot(a, vh)
                 + _dot_nt(qaf, (st * mu_f[:, kc]).astype(BF16))
                 + _dot_nt(qab, sbt_ref[0, ci, vc, :]))
            ksf = (kaf.astype(F32) * delta_f[:, kc]).astype(BF16)
            state_ref[h] = st * gamma_f[:, kc] + _dot_tn(vh, ksf)
            o = o * _rms_scale(o) * gln_ref[:, vc]
            yin_ref[rows, vc] = (o * sb_ref[0, rows, vc].astype(F32)).astype(BF16)

    y_b = _dot(yin_ref[...], wbo_ref[...])
    merged = (yam_ref[0].astype(F32) + smb_ref[0].astype(F32) * y_b).astype(BF16)
    y = x_ref[0] + mod_ref[0, 2:3, :] * _dot(merged, wo_ref[...])
    o_ref[0] = y * _rms_scale(y) * fng_ref[...]


def _post(x, qaf, kaf, qab, kab, v, sb, smb, yam, sbt, cvec, mod3, gln, wbo, wo, fng):
    B, L, _ = x.shape
    T = TILE
    nt = L // T
    nc = T // CHUNK
    tok = lambda w: pl.BlockSpec((1, T, w), lambda b, i: (b, i, 0))
    in_specs = [tok(D), tok(DK), tok(DK), tok(DK), tok(DK), tok(D), tok(D), tok(D), tok(D),
                pl.BlockSpec((1, nc, HEADS * HV, HK), lambda b, i: (b, i, 0, 0)),
                pl.BlockSpec((1, nc, 8, DK), lambda b, i: (b, i, 0, 0)),
                pl.BlockSpec((1, 3, D), lambda b, i: (b, 0, 0)),
                _const_spec((1, D)), _const_spec((D, D)), _const_spec((D, D)), _const_spec((1, D))]
    return pl.pallas_call(
        _post_kernel,
        grid=(B, nt),
        in_specs=in_specs,
        out_specs=tok(D),
        out_shape=jax.ShapeDtypeStruct((B, L, D), F32),
        scratch_shapes=[pltpu.VMEM((HEADS, HV, HK), F32), pltpu.VMEM((T, D), BF16)],
        compiler_params=pltpu.CompilerParams(
            dimension_semantics=("arbitrary", "arbitrary"), vmem_limit_bytes=VMEM_LIMIT),
        name="encoder_post",
    )(x, qaf, kaf, qab, kab, v, sb, smb, yam, sbt, cvec, mod3, gln, wbo, wo, fng)


def _slab_pair(wl, wr):
    k = wl.shape[0]
    pair = jnp.stack([wl.reshape(k, D // SLAB, SLAB), wr.reshape(k, D // SLAB, SLAB)], axis=2)
    return pair.reshape(k, 2 * D)


def kernel(x_prompt, x_sample, c_prompt, c_sample, w_ada, b_ada, norm_g, w_in, conv_w, conv_b,
           w_gate_f, b_gate_f, w_gate_b, b_gate_b, gla_norm_g, w_a_out, w_b_out, w_out,
           final_norm_g):
    assert w_ada.shape[0] == 1, "one layer"
    w = w_in[0].astype(BF16)
    o_q = 4 * D
    o_lr = o_q + 2 * DK + 2 * D
    o_m = o_lr + 2 * RANK
    wcx = _slab_pair(w[:, D:2 * D], w[:, 2 * D:3 * D])
    wbz = _slab_pair(w[:, :D], w[:, 3 * D:4 * D])
    wq = w[:, o_q:o_lr]
    wlr = jnp.pad(w[:, o_lr:o_m], ((0, 0), (0, 128 - 2 * RANK)))
    wm = w[:, o_m:]
    wg = jnp.zeros((128, 2 * DK), F32)
    wg = wg.at[:RANK, :DK].set(w_gate_f[0]).at[RANK:2 * RANK, DK:].set(w_gate_b[0]).astype(BF16)
    bg = jnp.concatenate([b_gate_f[0], b_gate_b[0]])[None, :]
    ng = norm_g[0][None, :]
    cw = conv_w[0]
    cb = conv_b[0][None, :]
    gln = gla_norm_g[0][None, :]
    wao = w_a_out[0].astype(BF16)
    wbo = w_b_out[0].astype(BF16)
    wo = w_out[0].astype(BF16)
    fng = final_norm_g[None, :]

    nb = x_prompt.shape[0]
    mod = _modulation(jnp.concatenate([c_prompt, c_sample], axis=0), w_ada[0], b_ada[0][None, :])
    mod3 = mod.reshape(mod.shape[0], 3, D)

    def encode(x, m3):
        pre = _pre(x, m3, ng, wcx, wbz, wq, wlr, wm, cw, cb, wg, bg, wao)
        return _post(x, *pre, m3, gln, wbo, wo, fng)

    return (encode(x_prompt, mod3[:nb]), encode(x_sample, mod3[nb:]))
```

```python
import functools

import jax
import jax.numpy as jnp
from jax import lax
from jax.experimental import pallas as pl
from jax.experimental.pallas import tpu as pltpu

D = 1024
HEADS = 4
DK = 512
HK = DK // HEADS
HV = D // HEADS
RANK = 16
GATE_TEMP = 16.0
EPS = 1e-6
CHUNK = 128
TILE = 512
HALO = 16
SLAB = 256
LOG2E = 1.4426950408889634
EXP2_CLAMP = 115.0
VMEM_LIMIT = 58 * 1024 * 1024

F32 = jnp.float32
BF16 = jnp.bfloat16


def _dot(a, b):
    return jnp.dot(a, b, preferred_element_type=F32)


def _dot_nt(a, b):
    return lax.dot_general(a, b, (((1,), (1,)), ((), ())), preferred_element_type=F32)


def _sigmoid(x):
    return 0.5 + 0.5 * jnp.tanh(0.5 * x)


def _silu(x):
    h = 0.5 * x
    return h + h * jnp.tanh(h)


def _rms_scale(x):
    return lax.rsqrt(jnp.mean(x * x, axis=-1, keepdims=True) + EPS)


def _const_spec(shape):
    zeros = (0,) * len(shape)
    return pl.BlockSpec(shape, lambda b, i: zeros, pipeline_mode=pl.Buffered(1))


def _mod_kernel(c_ref, w_ref, b_ref, o_ref):
    o_ref[...] = jnp.dot(_silu(c_ref[...]), w_ref[...], preferred_element_type=F32,
                         precision=lax.Precision.HIGHEST) + b_ref[...]


def _modulation(c, w_ada, b_ada):
    n = c.shape[0]
    return pl.pallas_call(
        _mod_kernel,
        out_shape=jax.ShapeDtypeStruct((n, 3 * D), F32),
        compiler_params=pltpu.CompilerParams(vmem_limit_bytes=VMEM_LIMIT),
        name="adaln_mod",
    )(c, w_ada, b_ada)


def _pre_kernel(x_ref, xp_ref, xn_ref, mod_ref, ng_ref, wcx_ref, wbz_ref, wq_ref, wlr_ref, wm_ref,
                cw_ref, cb_ref, wg_ref, bg_ref, wao_ref,
                qfb_ref, kft_ref, kbt_ref, v_ref, sb_ref, smb_ref, yam_ref,
                sbt_ref, uft_ref, cvec_ref, state_ref, ga_ref, *, n_tiles):
    i = pl.program_id(1)
    tile = n_tiles - 1 - i
    T = x_ref.shape[1]

    @pl.when(i == 0)
    def _():
        state_ref[...] = jnp.zeros_like(state_ref)

    shift = mod_ref[0, 0:1, :]
    scale1 = 1.0 + mod_ref[0, 1:2, :]
    ng = ng_ref[...]

    def normed(xf):
        return ((xf * _rms_scale(xf) * ng) * scale1 + shift).astype(BF16)

    hb = normed(x_ref[0])
    zero_halo = jnp.zeros((HALO, D), BF16)
    hprev = jnp.where(tile > 0, normed(xp_ref[0]), zero_halo)
    hnext = jnp.where(tile < n_tiles - 1, normed(xn_ref[0]), zero_halo)
    hx = jnp.concatenate([hprev, hb, hnext], axis=0)

    lr = _dot(hb, wlr_ref[...]).astype(BF16)
    qk = _dot(hb, wq_ref[...])
    z = _dot(lr, wg_ref[...]) + bg_ref[...]
    g2 = (jnp.minimum(z, 0.0) - jnp.log1p(jnp.exp(-jnp.abs(z)))) * (LOG2E / GATE_TEMP)

    q = qk[:, :DK] * (HK ** -0.5)
    k = qk[:, DK:2 * DK]
    vb = qk[:, 2 * DK:2 * DK + D].astype(BF16)
    v_ref[0] = vb
    vt = jnp.transpose(vb)
    sb_ref[0] = _silu(qk[:, 2 * DK + D:]).astype(BF16)
    smb_ref[0] = _sigmoid(_dot(hb, wm_ref[:, D:])).astype(BF16)

    r = lax.broadcasted_iota(jnp.int32, (CHUNK, 2 * CHUNK), 0)
    c = lax.broadcasted_iota(jnp.int32, (CHUNK, 2 * CHUNK), 1) % CHUNK
    tri_incl = (c <= r).astype(BF16)
    tri_strict = (c < r).astype(BF16)
    mid = CHUNK // 2
    hilo_of_chunk = {}

    def gla_piece(ci, hp):
        rows = slice(ci * CHUNK, (ci + 1) * CHUNK)
        if ci not in hilo_of_chunk:
            gc = g2[rows]
            hi = gc.astype(BF16)
            lo = (gc - hi.astype(F32)).astype(BF16)
            hilo_of_chunk[ci] = (jnp.concatenate([hi, lo], axis=0), gc)
        hilo, gc = hilo_of_chunk[ci]
        pc = slice(hp * 2 * HK, (hp + 1) * 2 * HK)
        pcb = slice(DK + hp * 2 * HK, DK + (hp + 1) * 2 * HK)
        qc, kc = q[rows, pc], k[rows, pc]
        gf = _dot(tri_incl, hilo[:, pc])
        gm = gf[mid - 1:mid, :]
        gl = gf[CHUNK - 1:CHUNK, :]
        ef = jnp.exp2(lax.clamp(-EXP2_CLAMP, gf - gm, EXP2_CLAMP))
        qaf = (qc * ef).astype(BF16)
        kaf = kc * (1.0 / ef)
        kft_ref[0, ci, pc, :] = jnp.transpose(kaf.astype(BF16))
        eb = _dot(tri_strict, hilo[:, pcb])
        em = eb[mid:mid + 1, :]
        et = eb[CHUNK - 1:CHUNK, :] + gc[CHUNK - 1:CHUNK, pcb]
        ebx = jnp.exp2(lax.clamp(-EXP2_CLAMP, em - eb, EXP2_CLAMP))
        qab = (qc * ebx).astype(BF16)
        kab = kc * (1.0 / ebx)
        kbt_ref[0, ci, pc, :] = jnp.transpose(kab.astype(BF16))
        cvec_ref[0, ci, :, pc] = jnp.concatenate(
            [jnp.exp2(gl), jnp.exp2(gm), jnp.zeros((6, 2 * HK), F32)], axis=0)

        mu_b = jnp.exp2(et - em)
        gamma_b = jnp.exp2(et)
        ksf = (kaf * jnp.exp2(gl - gm)).astype(BF16)
        ksb = (kab * jnp.exp2(em)).astype(BF16)
        for hh in range(2):
            h = 2 * hp + hh
            lc = slice(hh * HK, (hh + 1) * HK)
            vc = slice(h * HV, (h + 1) * HV)
            qfb_ref[0, rows, 2 * h * HK:(2 * h + 1) * HK] = qaf[:, lc]
            qfb_ref[0, rows, (2 * h + 1) * HK:(2 * h + 2) * HK] = qab[:, lc]
            u = _dot(vt[vc, rows], jnp.concatenate([ksf[:, lc], ksb[:, lc]], axis=1))
            uft_ref[0, ci, vc, :] = u[:, :HK]
            st = state_ref[h]
            sbt_ref[0, ci, vc, :] = (st * mu_b[:, lc]).astype(BF16)
            state_ref[h] = st * gamma_b[:, lc] + u[:, HK:]

    inner = slice(HALO, T + HALO)
    slab_vals = {}

    def conv_piece(s, part):
        cols = slice(s * SLAB, (s + 1) * SLAB)
        w0 = 2 * s * SLAB
        if part == 0:
            slab_vals[s, 'c'] = _dot(hx, wcx_ref[:, w0:w0 + SLAB])
        elif part == 1:
            p = slab_vals.pop((s, 'c')) * _dot(hx, wcx_ref[:, w0 + SLAB:w0 + 2 * SLAB])
            p_prev = pltpu.roll(p, 1, axis=0)[inner]
            p_next = pltpu.roll(p, T + 2 * HALO - 1, axis=0)[inner]
            slab_vals[s, 'u'] = (cw_ref[0:1, cols] * p_prev + cw_ref[1:2, cols] * p[inner]
                                 + cw_ref[2:3, cols] * p_next + cb_ref[:, cols])
        elif part == 2:
            slab_vals[s, 'bu'] = _dot(hb, wbz_ref[:, w0:w0 + SLAB]) * slab_vals.pop((s, 'u'))
        else:
            az = _dot(hb, wbz_ref[:, w0 + SLAB:w0 + 2 * SLAB])
            ga_ref[:, cols] = (slab_vals.pop((s, 'bu')) * _silu(az)).astype(BF16)

    gla_pieces = [(ci, hp) for ci in reversed(range(T // CHUNK)) for hp in range(HEADS // 2)]
    conv_pieces = [(s, part) for s in range(D // SLAB) for part in range(4)]
    per_gla = len(conv_pieces) // len(gla_pieces)
    for j, gp in enumerate(gla_pieces):
        gla_piece(*gp)
        for cp in conv_pieces[j * per_gla:(j + 1) * per_gla]:
            conv_piece(*cp)
    for cp in conv_pieces[len(gla_pieces) * per_gla:]:
        conv_piece(*cp)

    y_a = _dot(ga_ref[...], wao_ref[...])
    yam_ref[0] = (_sigmoid(_dot(hb, wm_ref[:, :D])) * y_a).astype(BF16)


def _pre(x, mod3, ng, wcx, wbz, wq, wlr, wm, cw, cb, wg, bg, wao):
    B, L, _ = x.shape
    T = TILE
    nt = L // T
    nc = T // CHUNK
    tok = lambda w: pl.BlockSpec((1, T, w), lambda b, i: (b, nt - 1 - i, 0))
    hpt = T // HALO
    in_specs = [
        tok(D),
        pl.BlockSpec((1, HALO, D), lambda b, i: (b, jnp.maximum((nt - 1 - i) * hpt - 1, 0), 0)),
        pl.BlockSpec((1, HALO, D), lambda b, i: (b, jnp.minimum((nt - i) * hpt, L // HALO - 1), 0)),
        pl.BlockSpec((1, 3, D), lambda b, i: (b, 0, 0)),
        _const_spec((1, D)),
        _const_spec((D, 2 * D)),
        _const_spec((D, 2 * D)),
        _const_spec((D, 3 * D)),
        _const_spec((D, 128)),
        _const_spec((D, 2 * D)),
        _const_spec((3, D)),
        _const_spec((1, D)),
        _const_spec((128, 2 * DK)),
        _const_spec((1, 2 * DK)),
        _const_spec((D, D)),
    ]
    out_shape = [
        jax.ShapeDtypeStruct((B, L, 2 * DK), BF16),
        jax.ShapeDtypeStruct((B, L // CHUNK, DK, CHUNK), BF16),
        jax.ShapeDtypeStruct((B, L // CHUNK, DK, CHUNK), BF16),
        jax.ShapeDtypeStruct((B, L, D), BF16),
        jax.ShapeDtypeStruct((B, L, D), BF16),
        jax.ShapeDtypeStruct((B, L, D), BF16),
        jax.ShapeDtypeStruct((B, L, D), BF16),
        jax.ShapeDtypeStruct((B, L // CHUNK, HEADS * HV, HK), BF16),
        jax.ShapeDtypeStruct((B, L // CHUNK, HEADS * HV, HK), F32),
        jax.ShapeDtypeStruct((B, L // CHUNK, 8, DK), F32),
    ]
    st_spec = pl.BlockSpec((1, nc, HEADS * HV, HK), lambda b, i: (b, nt - 1 - i, 0, 0))
    kt_spec = pl.BlockSpec((1, nc, DK, CHUNK), lambda b, i: (b, nt - 1 - i, 0, 0))
    out_specs = [tok(2 * DK), kt_spec, kt_spec, tok(D), tok(D), tok(D), tok(D), st_spec, st_spec,
                 pl.BlockSpec((1, nc, 8, DK), lambda b, i: (b, nt - 1 - i, 0, 0))]
    return pl.pallas_call(
        functools.partial(_pre_kernel, n_tiles=nt),
        grid=(B, nt),
        in_specs=in_specs,
        out_specs=out_specs,
        out_shape=out_shape,
        scratch_shapes=[pltpu.VMEM((HEADS, HV, HK), F32), pltpu.VMEM((T, D), BF16)],
        compiler_params=pltpu.CompilerParams(
            dimension_semantics=("arbitrary", "arbitrary"), vmem_limit_bytes=VMEM_LIMIT),
        name="encoder_pre",
    )(x, x, x, mod3, ng, wcx, wbz, wq, wlr, wm, cw, cb, wg, bg, wao)


def _post_kernel(x_ref, qfb_ref, kft_ref, kbt_ref, v_ref, sb_ref, smb_ref, yam_ref,
                 sbt_ref, uft_ref, cvec_ref, mod_ref, gln_ref, wbo_ref, wo_ref, fng_ref,
                 o_ref, state_ref, yin_ref):
    i = pl.program_id(1)
    T = x_ref.shape[1]
    n_chunks = T // CHUNK

    @pl.when(i == 0)
    def _():
        state_ref[...] = jnp.zeros_like(state_ref)

    r = lax.broadcasted_iota(jnp.int32, (CHUNK, CHUNK), 0)
    c = lax.broadcasted_iota(jnp.int32, (CHUNK, CHUNK), 1)
    causal = c <= r
    zeros = jnp.zeros((CHUNK, HK), BF16)

    att = {}
    for ci in range(n_chunks):
        rows = slice(ci * CHUNK, (ci + 1) * CHUNK)
        for h in range(HEADS):
            kc = slice(h * HK, (h + 1) * HK)
            kblk = jnp.concatenate(
                [jnp.concatenate([kft_ref[0, ci, kc, :], zeros], axis=1),
                 jnp.concatenate([zeros, kbt_ref[0, ci, kc, :]], axis=1)], axis=0)
            a2 = _dot(qfb_ref[0, rows, 2 * h * HK:(2 * h + 2) * HK], kblk)
            att[ci, h] = jnp.where(causal, a2[:, :CHUNK], a2[:, CHUNK:]).astype(BF16)

    scat = {}
    for h in range(HEADS):
        kc = slice(h * HK, (h + 1) * HK)
        vc = slice(h * HV, (h + 1) * HV)
        st = state_ref[h]
        for ci in range(n_chunks):
            cv = cvec_ref[0, ci]
            scat[ci, h] = jnp.concatenate(
                [(st * cv[1:2, kc]).astype(BF16), sbt_ref[0, ci, vc, :]], axis=1)
            st = st * cv[0:1, kc] + uft_ref[0, ci, vc, :]
        state_ref[h] = st

    for ci in range(n_chunks):
        rows = slice(ci * CHUNK, (ci + 1) * CHUNK)
        for h in range(HEADS):
            vc = slice(h * HV, (h + 1) * HV)
            o = (_dot(att[ci, h], v_ref[0, rows, vc])
                 + _dot_nt(qfb_ref[0, rows, 2 * h * HK:(2 * h + 2) * HK], scat[ci, h]))
            o = o * _rms_scale(o) * gln_ref[:, vc]
            yin_ref[rows, vc] = (o * sb_ref[0, rows, vc].astype(F32)).astype(BF16)

    y_b = _dot(yin_ref[...], wbo_ref[...])
    merged = (yam_ref[0].astype(F32) + smb_ref[0].astype(F32) * y_b).astype(BF16)
    y = x_ref[0] + mod_ref[0, 2:3, :] * _dot(merged, wo_ref[...])
    o_ref[0] = y * _rms_scale(y) * fng_ref[...]


def _post(x, qfb, kft, kbt, v, sb, smb, yam, sbt, uft, cvec, mod3, gln, wbo, wo, fng):
    B, L, _ = x.shape
    T = TILE
    nt = L // T
    nc = T // CHUNK
    tok = lambda w: pl.BlockSpec((1, T, w), lambda b, i: (b, i, 0))
    st_spec = pl.BlockSpec((1, nc, HEADS * HV, HK), lambda b, i: (b, i, 0, 0))
    kt_spec = pl.BlockSpec((1, nc, DK, CHUNK), lambda b, i: (b, i, 0, 0))
    in_specs = [tok(D), tok(2 * DK), kt_spec, kt_spec, tok(D), tok(D), tok(D), tok(D),
                st_spec, st_spec,
                pl.BlockSpec((1, nc, 8, DK), lambda b, i: (b, i, 0, 0)),
                pl.BlockSpec((1, 3, D), lambda b, i: (b, 0, 0)),
                _const_spec((1, D)), _const_spec((D, D)), _const_spec((D, D)), _const_spec((1, D))]
    return pl.pallas_call(
        _post_kernel,
        grid=(B, nt),
        in_specs=in_specs,
        out_specs=tok(D),
        out_shape=jax.ShapeDtypeStruct((B, L, D), F32),
        scratch_shapes=[pltpu.VMEM((HEADS, HV, HK), F32), pltpu.VMEM((T, D), BF16)],
        compiler_params=pltpu.CompilerParams(
            dimension_semantics=("arbitrary", "arbitrary"), vmem_limit_bytes=VMEM_LIMIT),
        name="encoder_post",
    )(x, qfb, kft, kbt, v, sb, smb, yam, sbt, uft, cvec, mod3, gln, wbo, wo, fng)


def _slab_pair(wl, wr):
    k = wl.shape[0]
    pair = jnp.stack([wl.reshape(k, D // SLAB, SLAB), wr.reshape(k, D // SLAB, SLAB)], axis=2)
    return pair.reshape(k, 2 * D)


def kernel(x_prompt, x_sample, c_prompt, c_sample, w_ada, b_ada, norm_g, w_in, conv_w, conv_b,
           w_gate_f, b_gate_f, w_gate_b, b_gate_b, gla_norm_g, w_a_out, w_b_out, w_out,
           final_norm_g):
    assert w_ada.shape[0] == 1, "one layer"
    w = w_in[0].astype(BF16)
    o_q = 4 * D
    o_lr = o_q + 2 * DK + 2 * D
    o_m = o_lr + 2 * RANK
    wcx = _slab_pair(w[:, D:2 * D], w[:, 2 * D:3 * D])
    wbz = _slab_pair(w[:, :D], w[:, 3 * D:4 * D])
    wq = w[:, o_q:o_lr]
    wlr = jnp.pad(w[:, o_lr:o_m], ((0, 0), (0, 128 - 2 * RANK)))
    wm = w[:, o_m:]
    wg = jnp.zeros((128, 2 * DK), F32)
    wg = wg.at[:RANK, :DK].set(w_gate_f[0]).at[RANK:2 * RANK, DK:].set(w_gate_b[0]).astype(BF16)
    bg = jnp.concatenate([b_gate_f[0], b_gate_b[0]])[None, :]
    ng = norm_g[0][None, :]
    cw = conv_w[0]
    cb = conv_b[0][None, :]
    gln = gla_norm_g[0][None, :]
    wao = w_a_out[0].astype(BF16)
    wbo = w_b_out[0].astype(BF16)
    wo = w_out[0].astype(BF16)
    fng = final_norm_g[None, :]

    nb = x_prompt.shape[0]
    mod = _modulation(jnp.concatenate([c_prompt, c_sample], axis=0), w_ada[0], b_ada[0][None, :])
    mod3 = mod.reshape(mod.shape[0], 3, D)

    def encode(x, m3):
        pre = _pre(x, m3, ng, wcx, wbz, wq, wlr, wm, cw, cb, wg, bg, wao)
        return _post(x, *pre, m3, gln, wbo, wo, fng)

    return (encode(x_prompt, mod3[:nb]), encode(x_sample, mod3[nb:]))
```

```python
import functools

import jax
import jax.numpy as jnp
from jax import lax
from jax.experimental import pallas as pl
from jax.experimental.pallas import tpu as pltpu

D = 1024
HEADS = 4
DK = 512
HK = DK // HEADS
HV = D // HEADS
RANK = 16
GATE_TEMP = 16.0
EPS = 1e-6
CHUNK = 128
TILE = 512
HALO = 16
SLAB = 256
LOG2E = 1.4426950408889634
EXP2_CLAMP = 115.0
VMEM_LIMIT = 58 * 1024 * 1024

F32 = jnp.float32
BF16 = jnp.bfloat16


def _dot(a, b):
    return jnp.dot(a, b, preferred_element_type=F32)


def _sigmoid(x):
    return 0.5 + 0.5 * jnp.tanh(0.5 * x)


def _silu(x):
    h = 0.5 * x
    return h + h * jnp.tanh(h)


def _rms_scale(x):
    return lax.rsqrt(jnp.mean(x * x, axis=-1, keepdims=True) + EPS)


def _const_spec(shape):
    zeros = (0,) * len(shape)
    return pl.BlockSpec(shape, lambda b, i: zeros, pipeline_mode=pl.Buffered(1))


def _mod_kernel(c_ref, w_ref, b_ref, o_ref):
    o_ref[...] = jnp.dot(_silu(c_ref[...]), w_ref[...], preferred_element_type=F32,
                         precision=lax.Precision.HIGHEST) + b_ref[...]


def _modulation(c, w_ada, b_ada):
    n = c.shape[0]
    return pl.pallas_call(
        _mod_kernel,
        out_shape=jax.ShapeDtypeStruct((n, 3 * D), F32),
        compiler_params=pltpu.CompilerParams(vmem_limit_bytes=VMEM_LIMIT),
        name="adaln_mod",
    )(c, w_ada, b_ada)


def _pre_kernel(x_ref, xp_ref, xn_ref, mod_ref, ng_ref, wab_ref, wac_ref, wax_ref, waz_ref,
                wqk_ref, wv_ref, wlr_ref, wma_ref,
                cw_ref, cb_ref, wg_ref, bg_ref, wao_ref,
                qfb_ref, kft_ref, kbt_ref, v_ref, hb_ref, yam_ref,
                sbs_ref, uft_ref, cvec_ref, state_ref, ga_ref, *, n_tiles):
    i = pl.program_id(1)
    tile = n_tiles - 1 - i
    T = x_ref.shape[1]

    @pl.when(i == 0)
    def _():
        state_ref[...] = jnp.zeros_like(state_ref)

    shift = mod_ref[0, 0:1, :]
    scale1 = 1.0 + mod_ref[0, 1:2, :]
    ng = ng_ref[...]

    def normed(xf):
        return ((xf * _rms_scale(xf) * ng) * scale1 + shift).astype(BF16)

    hb = normed(x_ref[0])
    hb_ref[0] = hb
    zero_halo = jnp.zeros((HALO, D), BF16)
    hprev = jnp.where(tile > 0, normed(xp_ref[0]), zero_halo)
    hnext = jnp.where(tile < n_tiles - 1, normed(xn_ref[0]), zero_halo)
    hx = jnp.concatenate([hprev, hb, hnext], axis=0)

    lr = _dot(hb, wlr_ref[...]).astype(BF16)
    qk = _dot(hb, wqk_ref[...])
    z = _dot(lr, wg_ref[...]) + bg_ref[...]
    g2 = (jnp.minimum(z, 0.0) - jnp.log1p(jnp.exp(-jnp.abs(z)))) * (LOG2E / GATE_TEMP)

    q = qk[:, :DK] * (HK ** -0.5)
    k = qk[:, DK:2 * DK]
    vb = _dot(hb, wv_ref[...]).astype(BF16)
    v_ref[0] = vb
    vt = jnp.transpose(vb)

    r = lax.broadcasted_iota(jnp.int32, (CHUNK, 2 * CHUNK), 0)
    c = lax.broadcasted_iota(jnp.int32, (CHUNK, 2 * CHUNK), 1) % CHUNK
    tri_incl = (c <= r).astype(BF16)
    tri_strict = (c < r).astype(BF16)
    mid = CHUNK // 2
    hilo_of_chunk = {}
    staged = {}

    def pair_cols(hp):
        return slice(hp * 2 * HK, (hp + 1) * 2 * HK), slice(DK + hp * 2 * HK, DK + (hp + 1) * 2 * HK)

    def gla_cumsum(ci, hp):
        rows = slice(ci * CHUNK, (ci + 1) * CHUNK)
        if ci not in hilo_of_chunk:
            gc = g2[rows]
            hi = gc.astype(BF16)
            lo = (gc - hi.astype(F32)).astype(BF16)
            hilo_of_chunk[ci] = (jnp.concatenate([hi, lo], axis=0), gc)
        hilo, _ = hilo_of_chunk[ci]
        pc, pcb = pair_cols(hp)
        staged[ci, hp, 'cum'] = (_dot(tri_incl, hilo[:, pc]), _dot(tri_strict, hilo[:, pcb]))

    def gla_operands(ci, hp):
        rows = slice(ci * CHUNK, (ci + 1) * CHUNK)
        pc, pcb = pair_cols(hp)
        gc = hilo_of_chunk[ci][1]
        gf, eb = staged.pop((ci, hp, 'cum'))
        qc, kc = q[rows, pc], k[rows, pc]
        gm = gf[mid - 1:mid, :]
        gl = gf[CHUNK - 1:CHUNK, :]
        ef = jnp.exp2(lax.clamp(-EXP2_CLAMP, gf - gm, EXP2_CLAMP))
        qaf = (qc * ef).astype(BF16)
        kaf = kc * (1.0 / ef)
        kft_ref[0, ci, pc, :] = jnp.transpose(kaf.astype(BF16))
        em = eb[mid:mid + 1, :]
        et = eb[CHUNK - 1:CHUNK, :] + gc[CHUNK - 1:CHUNK, pcb]
        ebx = jnp.exp2(lax.clamp(-EXP2_CLAMP, em - eb, EXP2_CLAMP))
        qab = (qc * ebx).astype(BF16)
        kab = kc * (1.0 / ebx)
        kbt_ref[0, ci, pc, :] = jnp.transpose(kab.astype(BF16))
        cvec_ref[0, ci, :, pc] = jnp.concatenate(
            [jnp.exp2(gl), jnp.exp2(gm), jnp.zeros((6, 2 * HK), F32)], axis=0)
        for hh in range(2):
            h = 2 * hp + hh
            lc = slice(hh * HK, (hh + 1) * HK)
            qfb_ref[0, rows, 2 * h * HK:(2 * h + 1) * HK] = qaf[:, lc]
            qfb_ref[0, rows, (2 * h + 1) * HK:(2 * h + 2) * HK] = qab[:, lc]
        staged[ci, hp, 'ks'] = ((kaf * jnp.exp2(gl - gm)).astype(BF16), (kab * jnp.exp2(em)).astype(BF16),
                                jnp.exp2(et - em), jnp.exp2(et))

    def gla_state(ci, hp):
        rows = slice(ci * CHUNK, (ci + 1) * CHUNK)
        ksf, ksb, mu_b, gamma_b = staged.pop((ci, hp, 'ks'))
        for hh in range(2):
            h = 2 * hp + hh
            lc = slice(hh * HK, (hh + 1) * HK)
            vc = slice(h * HV, (h + 1) * HV)
            u = _dot(vt[vc, rows], jnp.concatenate([ksf[:, lc], ksb[:, lc]], axis=1))
            uft_ref[0, ci, vc, :] = u[:, :HK].astype(BF16)
            st = state_ref[h]
            sbs_ref[0, ci, h * HK:(h + 1) * HK, :] = jnp.transpose(st * mu_b[:, lc]).astype(BF16)
            state_ref[h] = st * gamma_b[:, lc] + u[:, HK:]

    inner = slice(HALO, T + HALO)
    slab_vals = {}

    def conv_piece(s, part):
        cols = slice(s * SLAB, (s + 1) * SLAB)
        if part == 0:
            slab_vals[s, 'c'] = _dot(hx, wac_ref[:, cols])
        elif part == 1:
            p = slab_vals.pop((s, 'c')) * _dot(hx, wax_ref[:, cols])
            p_prev = pltpu.roll(p, 1, axis=0)[inner]
            p_next = pltpu.roll(p, T + 2 * HALO - 1, axis=0)[inner]
            slab_vals[s, 'u'] = (cw_ref[0:1, cols] * p_prev + cw_ref[1:2, cols] * p[inner]
                                 + cw_ref[2:3, cols] * p_next + cb_ref[:, cols])
        elif part == 2:
            slab_vals[s, 'bu'] = _dot(hb, wab_ref[:, cols]) * slab_vals.pop((s, 'u'))
        else:
            az = _dot(hb, waz_ref[:, cols])
            ga_ref[:, cols] = (slab_vals.pop((s, 'bu')) * _silu(az)).astype(BF16)

    gla_pieces = [(ci, hp) for ci in reversed(range(T // CHUNK)) for hp in range(HEADS // 2)]
    conv_pieces = [(s, part) for s in range(D // SLAB) for part in range(4)]
    assert len(conv_pieces) == 2 * len(gla_pieces)
    gla_cumsum(*gla_pieces[0])
    for j, gp in enumerate(gla_pieces):
        conv_piece(*conv_pieces[2 * j])
        gla_operands(*gp)
        if j + 1 < len(gla_pieces):
            gla_cumsum(*gla_pieces[j + 1])
        conv_piece(*conv_pieces[2 * j + 1])
        gla_state(*gp)

    m_a = _dot(hb, wma_ref[...])
    y_a = _dot(ga_ref[...], wao_ref[...])
    yam_ref[0] = (_sigmoid(m_a) * y_a).astype(BF16)


def _pre(x, mod3, ng, w, wlr, wma, cw, cb, wg, bg, wao):
    B, L, _ = x.shape
    T = TILE
    nt = L // T
    nc = T // CHUNK
    tok = lambda w: pl.BlockSpec((1, T, w), lambda b, i: (b, nt - 1 - i, 0))
    hpt = T // HALO
    wcol = lambda j: pl.BlockSpec((D, D), lambda b, i: (0, j), pipeline_mode=pl.Buffered(1))
    in_specs = [
        tok(D),
        pl.BlockSpec((1, HALO, D), lambda b, i: (b, jnp.maximum((nt - 1 - i) * hpt - 1, 0), 0)),
        pl.BlockSpec((1, HALO, D), lambda b, i: (b, jnp.minimum((nt - i) * hpt, L // HALO - 1), 0)),
        pl.BlockSpec((1, 3, D), lambda b, i: (b, 0, 0)),
        _const_spec((1, D)),
        wcol(0), wcol(1), wcol(2), wcol(3),
        wcol(4), wcol(5),
        _const_spec((D, 128)),
        _const_spec((D, D)),
        _const_spec((3, D)),
        _const_spec((1, D)),
        _const_spec((128, 2 * DK)),
        _const_spec((1, 2 * DK)),
        _const_spec((D, D)),
    ]
    out_shape = [
        jax.ShapeDtypeStruct((B, L, 2 * DK), BF16),
        jax.ShapeDtypeStruct((B, L // CHUNK, DK, CHUNK), BF16),
        jax.ShapeDtypeStruct((B, L // CHUNK, DK, CHUNK), BF16),
        jax.ShapeDtypeStruct((B, L, D), BF16),
        jax.ShapeDtypeStruct((B, L, D), BF16),
        jax.ShapeDtypeStruct((B, L, D), BF16),
        jax.ShapeDtypeStruct((B, L // CHUNK, DK, HV), BF16),
        jax.ShapeDtypeStruct((B, L // CHUNK, HEADS * HV, HK), BF16),
        jax.ShapeDtypeStruct((B, L // CHUNK, 8, DK), F32),
    ]
    st_spec = pl.BlockSpec((1, nc, HEADS * HV, HK), lambda b, i: (b, nt - 1 - i, 0, 0))
    kt_spec = pl.BlockSpec((1, nc, DK, CHUNK), lambda b, i: (b, nt - 1 - i, 0, 0))
    sb_spec = pl.BlockSpec((1, nc, DK, HV), lambda b, i: (b, nt - 1 - i, 0, 0))
    out_specs = [tok(2 * DK), kt_spec, kt_spec, tok(D), tok(D), tok(D), sb_spec, st_spec,
                 pl.BlockSpec((1, nc, 8, DK), lambda b, i: (b, nt - 1 - i, 0, 0))]
    return pl.pallas_call(
        functools.partial(_pre_kernel, n_tiles=nt),
        grid=(B, nt),
        in_specs=in_specs,
        out_specs=out_specs,
        out_shape=out_shape,
        scratch_shapes=[pltpu.VMEM((HEADS, HV, HK), F32), pltpu.VMEM((T, D), BF16)],
        compiler_params=pltpu.CompilerParams(
            dimension_semantics=("arbitrary", "arbitrary"), vmem_limit_bytes=VMEM_LIMIT),
        name="encoder_pre",
    )(x, x, x, mod3, ng, w, w, w, w, w, w, wlr, wma, cw, cb, wg, bg, wao)


def _post_kernel(x_ref, qfb_ref, kft_ref, kbt_ref, v_ref, hb_ref, yam_ref,
                 sbt_ref, uft_ref, cvec_ref, mod_ref, gln_ref, wbz_ref, wmb_ref, wbo_ref, wo_ref, fng_ref,
                 o_ref, state_ref, yin_ref):
    i = pl.program_id(1)
    T = x_ref.shape[1]
    n_chunks = T // CHUNK

    @pl.when(i == 0)
    def _():
        state_ref[...] = jnp.zeros_like(state_ref)

    r = lax.broadcasted_iota(jnp.int32, (CHUNK, CHUNK), 0)
    c = lax.broadcasted_iota(jnp.int32, (CHUNK, CHUNK), 1)
    causal = c <= r
    zeros = jnp.zeros((CHUNK, HK), BF16)

    att = {}
    for ci in range(n_chunks):
        rows = slice(ci * CHUNK, (ci + 1) * CHUNK)
        for h in range(HEADS):
            kc = slice(h * HK, (h + 1) * HK)
            kblk = jnp.concatenate(
                [jnp.concatenate([kft_ref[0, ci, kc, :], zeros], axis=1),
                 jnp.concatenate([zeros, kbt_ref[0, ci, kc, :]], axis=1)], axis=0)
            a2 = _dot(qfb_ref[0, rows, 2 * h * HK:(2 * h + 2) * HK], kblk)
            att[ci, h] = jnp.where(causal, a2[:, :CHUNK], a2[:, CHUNK:]).astype(BF16)

    sb = _silu(_dot(hb_ref[0], wbz_ref[...]))

    scat = {}
    for h in range(HEADS):
        kc = slice(h * HK, (h + 1) * HK)
        vc = slice(h * HV, (h + 1) * HV)
        st = state_ref[h]
        for ci in range(n_chunks):
            cv = cvec_ref[0, ci]
            scat[ci, h] = jnp.concatenate(
                [jnp.transpose(st * cv[1:2, kc]).astype(BF16), sbt_ref[0, ci, kc, :]], axis=0)
            st = st * cv[0:1, kc] + uft_ref[0, ci, vc, :].astype(F32)
        state_ref[h] = st

    for ci in range(n_chunks):
        rows = slice(ci * CHUNK, (ci + 1) * CHUNK)
        for h in range(HEADS):
            vc = slice(h * HV, (h + 1) * HV)
            o = (_dot(att[ci, h], v_ref[0, rows, vc])
                 + _dot(qfb_ref[0, rows, 2 * h * HK:(2 * h + 2) * HK], scat[ci, h]))
            o = o * _rms_scale(o) * gln_ref[:, vc]
            yin_ref[rows, vc] = (o * sb[rows, vc]).astype(BF16)

    smb = _sigmoid(_dot(hb_ref[0], wmb_ref[...]))
    half = T // 2
    merged = {}
    for hf in range(2):
        rows = slice(hf * half, (hf + 1) * half)
        y_b = _dot(yin_ref[rows, :], wbo_ref[...])
        merged[hf] = (yam_ref[0, rows, :].astype(F32) + smb[rows] * y_b).astype(BF16)
    for hf in range(2):
        rows = slice(hf * half, (hf + 1) * half)
        y = x_ref[0, rows, :] + mod_ref[0, 2:3, :] * _dot(merged[hf], wo_ref[...])
        o_ref[0, rows, :] = y * _rms_scale(y) * fng_ref[...]


def _post(x, qfb, kft, kbt, v, hb, yam, sbt, uft, cvec, mod3, gln, wbz, wmb, wbo, wo, fng):
    B, L, _ = x.shape
    T = TILE
    nt = L // T
    nc = T // CHUNK
    tok = lambda w: pl.BlockSpec((1, T, w), lambda b, i: (b, i, 0))
    st_spec = pl.BlockSpec((1, nc, HEADS * HV, HK), lambda b, i: (b, i, 0, 0))
    kt_spec = pl.BlockSpec((1, nc, DK, CHUNK), lambda b, i: (b, i, 0, 0))
    sb_spec = pl.BlockSpec((1, nc, DK, HV), lambda b, i: (b, i, 0, 0))
    in_specs = [tok(D), tok(2 * DK), kt_spec, kt_spec, tok(D), tok(D), tok(D),
                sb_spec, st_spec,
                pl.BlockSpec((1, nc, 8, DK), lambda b, i: (b, i, 0, 0)),
                pl.BlockSpec((1, 3, D), lambda b, i: (b, 0, 0)),
                _const_spec((1, D)), _const_spec((D, D)), _const_spec((D, D)), _const_spec((D, D)),
                _const_spec((D, D)), _const_spec((1, D))]
    return pl.pallas_call(
        _post_kernel,
        grid=(B, nt),
        in_specs=in_specs,
        out_specs=tok(D),
        out_shape=jax.ShapeDtypeStruct((B, L, D), F32),
        scratch_shapes=[pltpu.VMEM((HEADS, HV, HK), F32), pltpu.VMEM((T, D), BF16)],
        compiler_params=pltpu.CompilerParams(
            dimension_semantics=("arbitrary", "arbitrary"), vmem_limit_bytes=VMEM_LIMIT),
        name="encoder_post",
    )(x, qfb, kft, kbt, v, hb, yam, sbt, uft, cvec, mod3, gln, wbz, wmb, wbo, wo, fng)


def kernel(x_prompt, x_sample, c_prompt, c_sample, w_ada, b_ada, norm_g, w_in, conv_w, conv_b,
           w_gate_f, b_gate_f, w_gate_b, b_gate_b, gla_norm_g, w_a_out, w_b_out, w_out,
           final_norm_g):
    assert w_ada.shape[0] == 1, "one layer"
    w = w_in[0].astype(BF16)
    o_q = 4 * D
    o_lr = o_q + 2 * DK + 2 * D
    o_m = o_lr + 2 * RANK
    wlr = jnp.pad(w[:, o_lr:o_m], ((0, 0), (0, 128 - 2 * RANK)))
    wbz = w[:, o_q + 2 * DK + D:o_lr]
    wma = w[:, o_m:o_m + D]
    wmb = w[:, o_m + D:]
    wg = jnp.zeros((128, 2 * DK), F32)
    wg = wg.at[:RANK, :DK].set(w_gate_f[0]).at[RANK:2 * RANK, DK:].set(w_gate_b[0]).astype(BF16)
    bg = jnp.concatenate([b_gate_f[0], b_gate_b[0]])[None, :]
    ng = norm_g[0][None, :]
    cw = conv_w[0]
    cb = conv_b[0][None, :]
    gln = gla_norm_g[0][None, :]
    wao = w_a_out[0].astype(BF16)
    wbo = w_b_out[0].astype(BF16)
    wo = w_out[0].astype(BF16)
    fng = final_norm_g[None, :]

    nb = x_prompt.shape[0]
    mod = _modulation(jnp.concatenate([c_prompt, c_sample], axis=0), w_ada[0], b_ada[0][None, :])
    mod3 = mod.reshape(mod.shape[0], 3, D)

    def encode(x, m3):
        pre = _pre(x, m3, ng, w, wlr, wma, cw, cb, wg, bg, wao)
        return _post(x, *pre, m3, gln, wbz, wmb, wbo, wo, fng)

    return (encode(x_prompt, mod3[:nb]), encode(x_sample, mod3[nb:]))
```

```python
import functools

import jax
import jax.numpy as jnp
from jax import lax
from jax.experimental import pallas as pl
from jax.experimental.pallas import tpu as pltpu

D = 1024
HEADS = 4
DK = 512
HK = DK // HEADS
HV = D // HEADS
RANK = 16
GATE_TEMP = 16.0
EPS = 1e-6
CHUNK = 128
TILE = 512
HALO = 16
SLAB = 256
LOG2E = 1.4426950408889634
EXP2_CLAMP = 115.0
VMEM_LIMIT = 58 * 1024 * 1024

F32 = jnp.float32
BF16 = jnp.bfloat16


def _dot(a, b):
    return jnp.dot(a, b, preferred_element_type=F32)


def _wb(w_u32):
    return pltpu.bitcast(w_u32, BF16)


def _sigmoid(x):
    return 0.5 + 0.5 * jnp.tanh(0.5 * x)


def _silu(x):
    h = 0.5 * x
    return h + h * jnp.tanh(h)


def _rms_scale(x):
    return lax.rsqrt(jnp.mean(x * x, axis=-1, keepdims=True) + EPS)


def _const_spec(shape):
    zeros = (0,) * len(shape)
    return pl.BlockSpec(shape, lambda b, i: zeros, pipeline_mode=pl.Buffered(1))


def _mod_kernel(c_ref, w_ref, b_ref, o_ref):
    o_ref[...] = jnp.dot(_silu(c_ref[...]), w_ref[...], preferred_element_type=F32,
                         precision=lax.Precision.HIGHEST) + b_ref[...]


def _modulation(c, w_ada, b_ada):
    n = c.shape[0]
    return pl.pallas_call(
        _mod_kernel,
        out_shape=jax.ShapeDtypeStruct((n, 3 * D), F32),
        compiler_params=pltpu.CompilerParams(vmem_limit_bytes=VMEM_LIMIT),
        name="adaln_mod",
    )(c, w_ada, b_ada)


def _pre_kernel(x_ref, xp_ref, xn_ref, mod_ref, ng_ref, wab_ref, wac_ref, wax_ref, waz_ref,
                wqk_ref, wv_ref, wlr_ref, wma_ref,
                cw_ref, cb_ref, wg_ref, bg_ref, wao_ref,
                qfb_ref, kft_ref, kbt_ref, v_ref, hb_ref, yam_ref,
                sbs_ref, uft_ref, cvec_ref, state_ref, ga_ref, *, n_tiles):
    i = pl.program_id(1)
    tile = n_tiles - 1 - i
    T = x_ref.shape[1]

    @pl.when(i == 0)
    def _():
        state_ref[...] = jnp.zeros_like(state_ref)

    shift = mod_ref[0, 0:1, :]
    scale1 = 1.0 + mod_ref[0, 1:2, :]
    ng = ng_ref[...]

    def normed(xf):
        return ((xf * _rms_scale(xf) * ng) * scale1 + shift).astype(BF16)

    hb = normed(x_ref[0])
    hb_ref[0] = hb
    zero_halo = jnp.zeros((HALO, D), BF16)
    hprev = jnp.where(tile > 0, normed(xp_ref[0]), zero_halo)
    hnext = jnp.where(tile < n_tiles - 1, normed(xn_ref[0]), zero_halo)
    hx = jnp.concatenate([hprev, hb, hnext], axis=0)

    lr = _dot(hb, wlr_ref[...]).astype(BF16)
    qk = _dot(hb, _wb(wqk_ref[...]))
    z = _dot(lr, wg_ref[...]) + bg_ref[...]
    g2 = (jnp.minimum(z, 0.0) - jnp.log1p(jnp.exp(-jnp.abs(z)))) * (LOG2E / GATE_TEMP)

    q = qk[:, :DK] * (HK ** -0.5)
    k = qk[:, DK:2 * DK]
    vb = _dot(hb, _wb(wv_ref[...])).astype(BF16)
    v_ref[0] = vb
    vt = jnp.transpose(vb)

    r = lax.broadcasted_iota(jnp.int32, (CHUNK, 2 * CHUNK), 0)
    c = lax.broadcasted_iota(jnp.int32, (CHUNK, 2 * CHUNK), 1) % CHUNK
    tri_incl = (c <= r).astype(BF16)
    tri_strict = (c < r).astype(BF16)
    mid = CHUNK // 2
    hilo_of_chunk = {}
    staged = {}

    def pair_cols(hp):
        return slice(hp * 2 * HK, (hp + 1) * 2 * HK), slice(DK + hp * 2 * HK, DK + (hp + 1) * 2 * HK)

    def gla_cumsum(ci, hp):
        rows = slice(ci * CHUNK, (ci + 1) * CHUNK)
        if ci not in hilo_of_chunk:
            gc = g2[rows]
            hi = gc.astype(BF16)
            lo = (gc - hi.astype(F32)).astype(BF16)
            hilo_of_chunk[ci] = (jnp.concatenate([hi, lo], axis=0), gc)
        hilo, _ = hilo_of_chunk[ci]
        pc, pcb = pair_cols(hp)
        staged[ci, hp, 'cum'] = (_dot(tri_incl, hilo[:, pc]), _dot(tri_strict, hilo[:, pcb]))

    def gla_operands(ci, hp):
        rows = slice(ci * CHUNK, (ci + 1) * CHUNK)
        pc, pcb = pair_cols(hp)
        gc = hilo_of_chunk[ci][1]
        gf, eb = staged.pop((ci, hp, 'cum'))
        qc, kc = q[rows, pc], k[rows, pc]
        gm = gf[mid - 1:mid, :]
        gl = gf[CHUNK - 1:CHUNK, :]
        ef = jnp.exp2(lax.clamp(-EXP2_CLAMP, gf - gm, EXP2_CLAMP))
        qaf = (qc * ef).astype(BF16)
        kaf = kc * (1.0 / ef)
        kft_ref[0, ci, pc, :] = jnp.transpose(kaf.astype(BF16))
        em = eb[mid:mid + 1, :]
        et = eb[CHUNK - 1:CHUNK, :] + gc[CHUNK - 1:CHUNK, pcb]
        ebx = jnp.exp2(lax.clamp(-EXP2_CLAMP, em - eb, EXP2_CLAMP))
        qab = (qc * ebx).astype(BF16)
        kab = kc * (1.0 / ebx)
        kbt_ref[0, ci, pc, :] = jnp.transpose(kab.astype(BF16))
        cvec_ref[0, ci, :, pc] = jnp.concatenate(
            [jnp.exp2(gl), jnp.exp2(gm), jnp.zeros((6, 2 * HK), F32)], axis=0)
        for hh in range(2):
            h = 2 * hp + hh
            lc = slice(hh * HK, (hh + 1) * HK)
            qfb_ref[0, rows, 2 * h * HK:(2 * h + 1) * HK] = qaf[:, lc]
            qfb_ref[0, rows, (2 * h + 1) * HK:(2 * h + 2) * HK] = qab[:, lc]
        staged[ci, hp, 'ks'] = ((kaf * jnp.exp2(gl - gm)).astype(BF16), (kab * jnp.exp2(em)).astype(BF16),
                                jnp.exp2(et - em), jnp.exp2(et))

    def gla_state(ci, hp):
        rows = slice(ci * CHUNK, (ci + 1) * CHUNK)
        ksf, ksb, mu_b, gamma_b = staged.pop((ci, hp, 'ks'))
        for hh in range(2):
            h = 2 * hp + hh
            lc = slice(hh * HK, (hh + 1) * HK)
            vc = slice(h * HV, (h + 1) * HV)
            u = _dot(vt[vc, rows], jnp.concatenate([ksf[:, lc], ksb[:, lc]], axis=1))
            uft_ref[0, ci, vc, :] = u[:, :HK].astype(BF16)
            st = state_ref[h]
            sbs_ref[0, ci, h * HK:(h + 1) * HK, :] = jnp.transpose(st * mu_b[:, lc]).astype(BF16)
            state_ref[h] = st * gamma_b[:, lc] + u[:, HK:]

    inner = slice(HALO, T + HALO)
    slab_vals = {}

    def conv_piece(s, part):
        cols = slice(s * SLAB, (s + 1) * SLAB)
        if part == 0:
            slab_vals[s, 'c'] = _dot(hx, _wb(wac_ref[:, cols]))
        elif part == 1:
            p = slab_vals.pop((s, 'c')) * _dot(hx, _wb(wax_ref[:, cols]))
            p_prev = pltpu.roll(p, 1, axis=0)[inner]
            p_next = pltpu.roll(p, T + 2 * HALO - 1, axis=0)[inner]
            slab_vals[s, 'u'] = (cw_ref[0:1, cols] * p_prev + cw_ref[1:2, cols] * p[inner]
                                 + cw_ref[2:3, cols] * p_next + cb_ref[:, cols])
        elif part == 2:
            slab_vals[s, 'bu'] = _dot(hb, _wb(wab_ref[:, cols])) * slab_vals.pop((s, 'u'))
        else:
            az = _dot(hb, _wb(waz_ref[:, cols]))
            ga_ref[:, cols] = (slab_vals.pop((s, 'bu')) * _silu(az)).astype(BF16)

    gla_pieces = [(ci, hp) for ci in reversed(range(T // CHUNK)) for hp in range(HEADS // 2)]
    conv_pieces = [(s, part) for s in range(D // SLAB) for part in range(4)]
    assert len(conv_pieces) == 2 * len(gla_pieces)
    gla_cumsum(*gla_pieces[0])
    for j, gp in enumerate(gla_pieces):
        conv_piece(*conv_pieces[2 * j])
        gla_operands(*gp)
        if j + 1 < len(gla_pieces):
            gla_cumsum(*gla_pieces[j + 1])
        conv_piece(*conv_pieces[2 * j + 1])
        gla_state(*gp)

    for hc in range(2):
        cols = slice(hc * D // 2, (hc + 1) * D // 2)
        m_a = _dot(hb, _wb(wma_ref[:, cols]))
        y_a = _dot(ga_ref[...], _wb(wao_ref[:, cols]))
        yam_ref[0, :, cols] = (_sigmoid(m_a) * y_a).astype(BF16)


def _pre(x, mod3, ng, wp, wlr, wma, cw, cb, wg, bg, wao):
    B, L, _ = x.shape
    T = TILE
    nt = L // T
    nc = T // CHUNK
    tok = lambda w: pl.BlockSpec((1, T, w), lambda b, i: (b, nt - 1 - i, 0))
    hpt = T // HALO
    wcol = lambda j: pl.BlockSpec((D // 2, D), lambda b, i: (0, j), pipeline_mode=pl.Buffered(1))
    in_specs = [
        tok(D),
        pl.BlockSpec((1, HALO, D), lambda b, i: (b, jnp.maximum((nt - 1 - i) * hpt - 1, 0), 0)),
        pl.BlockSpec((1, HALO, D), lambda b, i: (b, jnp.minimum((nt - i) * hpt, L // HALO - 1), 0)),
        pl.BlockSpec((1, 3, D), lambda b, i: (b, 0, 0)),
        _const_spec((1, D)),
        wcol(0), wcol(1), wcol(2), wcol(3),
        wcol(4), wcol(5),
        _const_spec((D, 128)),
        _const_spec((D // 2, D)),
        _const_spec((3, D)),
        _const_spec((1, D)),
        _const_spec((128, 2 * DK)),
        _const_spec((1, 2 * DK)),
        _const_spec((D // 2, D)),
    ]
    out_shape = [
        jax.ShapeDtypeStruct((B, L, 2 * DK), BF16),
        jax.ShapeDtypeStruct((B, L // CHUNK, DK, CHUNK), BF16),
        jax.ShapeDtypeStruct((B, L // CHUNK, DK, CHUNK), BF16),
        jax.ShapeDtypeStruct((B, L, D), BF16),
        jax.ShapeDtypeStruct((B, L, D), BF16),
        jax.ShapeDtypeStruct((B, L, D), BF16),
        jax.ShapeDtypeStruct((B, L // CHUNK, DK, HV), BF16),
        jax.ShapeDtypeStruct((B, L // CHUNK, HEADS * HV, HK), BF16),
        jax.ShapeDtypeStruct((B, L // CHUNK, 8, DK), F32),
    ]
    st_spec = pl.BlockSpec((1, nc, HEADS * HV, HK), lambda b, i: (b, nt - 1 - i, 0, 0))
    kt_spec = pl.BlockSpec((1, nc, DK, CHUNK), lambda b, i: (b, nt - 1 - i, 0, 0))
    sb_spec = pl.BlockSpec((1, nc, DK, HV), lambda b, i: (b, nt - 1 - i, 0, 0))
    out_specs = [tok(2 * DK), kt_spec, kt_spec, tok(D), tok(D), tok(D), sb_spec, st_spec,
                 pl.BlockSpec((1, nc, 8, DK), lambda b, i: (b, nt - 1 - i, 0, 0))]
    return pl.pallas_call(
        functools.partial(_pre_kernel, n_tiles=nt),
        grid=(B, nt),
        in_specs=in_specs,
        out_specs=out_specs,
        out_shape=out_shape,
        scratch_shapes=[pltpu.VMEM((HEADS, HV, HK), F32), pltpu.VMEM((T, D), BF16)],
        compiler_params=pltpu.CompilerParams(
            dimension_semantics=("arbitrary", "arbitrary"), vmem_limit_bytes=VMEM_LIMIT),
        name="encoder_pre",
    )(x, x, x, mod3, ng, wp, wp, wp, wp, wp, wp, wlr, wma, cw, cb, wg, bg, wao)


def _post_kernel(x_ref, qfb_ref, kft_ref, kbt_ref, v_ref, hb_ref, yam_ref,
                 sbt_ref, uft_ref, cvec_ref, mod_ref, gln_ref, wbz_ref, wmb_ref, wbo_ref, wo_ref, fng_ref,
                 o_ref, state_ref, yin_ref):
    i = pl.program_id(1)
    T = x_ref.shape[1]
    n_chunks = T // CHUNK

    @pl.when(i == 0)
    def _():
        state_ref[...] = jnp.zeros_like(state_ref)

    r = lax.broadcasted_iota(jnp.int32, (CHUNK, CHUNK), 0)
    c = lax.broadcasted_iota(jnp.int32, (CHUNK, CHUNK), 1)
    causal = c <= r
    zeros = jnp.zeros((CHUNK, HK), BF16)

    att = {}
    for ci in range(n_chunks):
        rows = slice(ci * CHUNK, (ci + 1) * CHUNK)
        for h in range(HEADS):
            kc = slice(h * HK, (h + 1) * HK)
            kblk = jnp.concatenate(
                [jnp.concatenate([kft_ref[0, ci, kc, :], zeros], axis=1),
                 jnp.concatenate([zeros, kbt_ref[0, ci, kc, :]], axis=1)], axis=0)
            a2 = _dot(qfb_ref[0, rows, 2 * h * HK:(2 * h + 2) * HK], kblk)
            att[ci, h] = jnp.where(causal, a2[:, :CHUNK], a2[:, CHUNK:]).astype(BF16)

    sb = _silu(_dot(hb_ref[0], _wb(wbz_ref[...])))

    scat = {}
    for h in range(HEADS):
        kc = slice(h * HK, (h + 1) * HK)
        vc = slice(h * HV, (h + 1) * HV)
        st = state_ref[h]
        for ci in range(n_chunks):
            cv = cvec_ref[0, ci]
            scat[ci, h] = jnp.concatenate(
                [jnp.transpose(st * cv[1:2, kc]).astype(BF16), sbt_ref[0, ci, kc, :]], axis=0)
            st = st * cv[0:1, kc] + uft_ref[0, ci, vc, :].astype(F32)
        state_ref[h] = st

    for ci in range(n_chunks):
        rows = slice(ci * CHUNK, (ci + 1) * CHUNK)
        for h in range(HEADS):
            vc = slice(h * HV, (h + 1) * HV)
            o = (_dot(att[ci, h], v_ref[0, rows, vc])
                 + _dot(qfb_ref[0, rows, 2 * h * HK:(2 * h + 2) * HK], scat[ci, h]))
            o = o * _rms_scale(o) * gln_ref[:, vc]
            yin_ref[rows, vc] = (o * sb[rows, vc]).astype(BF16)

    smb = _sigmoid(_dot(hb_ref[0], _wb(wmb_ref[...])))
    half = T // 2
    merged = {}
    for hf in range(2):
        rows = slice(hf * half, (hf + 1) * half)
        y_b = _dot(yin_ref[rows, :], _wb(wbo_ref[...]))
        merged[hf] = (yam_ref[0, rows, :].astype(F32) + smb[rows] * y_b).astype(BF16)
    for hf in range(2):
        rows = slice(hf * half, (hf + 1) * half)
        y = x_ref[0, rows, :] + mod_ref[0, 2:3, :] * _dot(merged[hf], _wb(wo_ref[...]))
        o_ref[0, rows, :] = y * _rms_scale(y) * fng_ref[...]


def _post(x, qfb, kft, kbt, v, hb, yam, sbt, uft, cvec, mod3, gln, wbz, wmb, wbo, wo, fng):
    B, L, _ = x.shape
    T = TILE
    nt = L // T
    nc = T // CHUNK
    tok = lambda w: pl.BlockSpec((1, T, w), lambda b, i: (b, i, 0))
    st_spec = pl.BlockSpec((1, nc, HEADS * HV, HK), lambda b, i: (b, i, 0, 0))
    kt_spec = pl.BlockSpec((1, nc, DK, CHUNK), lambda b, i: (b, i, 0, 0))
    sb_spec = pl.BlockSpec((1, nc, DK, HV), lambda b, i: (b, i, 0, 0))
    in_specs = [tok(D), tok(2 * DK), kt_spec, kt_spec, tok(D), tok(D), tok(D),
                sb_spec, st_spec,
                pl.BlockSpec((1, nc, 8, DK), lambda b, i: (b, i, 0, 0)),
                pl.BlockSpec((1, 3, D), lambda b, i: (b, 0, 0)),
                _const_spec((1, D)), _const_spec((D // 2, D)), _const_spec((D // 2, D)),
                _const_spec((D // 2, D)), _const_spec((D // 2, D)), _const_spec((1, D))]
    return pl.pallas_call(
        _post_kernel,
        grid=(B, nt),
        in_specs=in_specs,
        out_specs=tok(D),
        out_shape=jax.ShapeDtypeStruct((B, L, D), F32),
        scratch_shapes=[pltpu.VMEM((HEADS, HV, HK), F32), pltpu.VMEM((T, D), BF16)],
        compiler_params=pltpu.CompilerParams(
            dimension_semantics=("arbitrary", "arbitrary"), vmem_limit_bytes=VMEM_LIMIT),
        name="encoder_post",
    )(x, qfb, kft, kbt, v, hb, yam, sbt, uft, cvec, mod3, gln, wbz, wmb, wbo, wo, fng)


def _pack_row_pairs(wb):
    k, n = wb.shape
    pairs = wb.reshape(k // 2, 2, n).swapaxes(1, 2)
    return lax.bitcast_convert_type(pairs, jnp.uint32)


def kernel(x_prompt, x_sample, c_prompt, c_sample, w_ada, b_ada, norm_g, w_in, conv_w, conv_b,
           w_gate_f, b_gate_f, w_gate_b, b_gate_b, gla_norm_g, w_a_out, w_b_out, w_out,
           final_norm_g):
    assert w_ada.shape[0] == 1, "one layer"
    w = w_in[0].astype(BF16)
    o_q = 4 * D
    o_lr = o_q + 2 * DK + 2 * D
    o_m = o_lr + 2 * RANK
    wp = _pack_row_pairs(w[:, :6 * D])
    wlr = jnp.pad(w[:, o_lr:o_m], ((0, 0), (0, 128 - 2 * RANK)))
    wbz = _pack_row_pairs(w[:, o_q + 2 * DK + D:o_lr])
    wma = _pack_row_pairs(w[:, o_m:o_m + D])
    wmb = _pack_row_pairs(w[:, o_m + D:])
    wg = jnp.zeros((128, 2 * DK), F32)
    wg = wg.at[:RANK, :DK].set(w_gate_f[0]).at[RANK:2 * RANK, DK:].set(w_gate_b[0]).astype(BF16)
    bg = jnp.concatenate([b_gate_f[0], b_gate_b[0]])[None, :]
    ng = norm_g[0][None, :]
    cw = conv_w[0]
    cb = conv_b[0][None, :]
    gln = gla_norm_g[0][None, :]
    wao = _pack_row_pairs(w_a_out[0].astype(BF16))
    wbo = _pack_row_pairs(w_b_out[0].astype(BF16))
    wo = _pack_row_pairs(w_out[0].astype(BF16))
    fng = final_norm_g[None, :]

    nb = x_prompt.shape[0]
    mod = _modulation(jnp.concatenate([c_prompt, c_sample], axis=0), w_ada[0], b_ada[0][None, :])
    mod3 = mod.reshape(mod.shape[0], 3, D)

    def encode(x, m3):
        pre = _pre(x, m3, ng, wp, wlr, wma, cw, cb, wg, bg, wao)
        return _post(x, *pre, m3, gln, wbz, wmb, wbo, wo, fng)

    return (encode(x_prompt, mod3[:nb]), encode(x_sample, mod3[nb:]))
```

```python
import functools

import jax
import jax.numpy as jnp
from jax import lax
from jax.experimental import pallas as pl
from jax.experimental.pallas import tpu as pltpu

D = 1024
HEADS = 4
DK = 512
HK = DK // HEADS
HV = D // HEADS
RANK = 16
GATE_TEMP = 16.0
EPS = 1e-6
CHUNK = 128
TILE = 512
HALO = 16
SLAB = 256
LOG2E = 1.4426950408889634
EXP2_CLAMP = 115.0
VMEM_LIMIT = 58 * 1024 * 1024

F32 = jnp.float32
BF16 = jnp.bfloat16


def _dot(a, b):
    return jnp.dot(a, b, preferred_element_type=F32)


def _wb(w_u32):
    return pltpu.bitcast(w_u32, BF16)


def _sigmoid(x):
    return 0.5 + 0.5 * jnp.tanh(0.5 * x)


def _silu(x):
    h = 0.5 * x
    return h + h * jnp.tanh(h)


def _rms_scale(x):
    return lax.rsqrt(jnp.mean(x * x, axis=-1, keepdims=True) + EPS)


def _const_spec(shape):
    zeros = (0,) * len(shape)
    return pl.BlockSpec(shape, lambda b, i: zeros, pipeline_mode=pl.Buffered(1))


def _mod_kernel(c_ref, w_ref, b_ref, o_ref):
    o_ref[...] = jnp.dot(_silu(c_ref[...]), w_ref[...], preferred_element_type=F32,
                         precision=lax.Precision.HIGHEST) + b_ref[...]


def _modulation(c, w_ada, b_ada):
    n = c.shape[0]
    return pl.pallas_call(
        _mod_kernel,
        out_shape=jax.ShapeDtypeStruct((n, 3 * D), F32),
        compiler_params=pltpu.CompilerParams(vmem_limit_bytes=VMEM_LIMIT),
        name="adaln_mod",
    )(c, w_ada, b_ada)


def _pre_kernel(x_ref, xp_ref, xn_ref, mod_ref, ng_ref, wab_ref, wac_ref, wax_ref, waz_ref,
                wqk_ref, wv_ref, wlr_ref, wma_ref,
                cw_ref, cb_ref, wg_ref, bg_ref, wao_ref,
                qfb_ref, kft_ref, kbt_ref, v_ref, hb_ref, yam_ref,
                sbs_ref, uft_ref, cvec_ref, state_ref, ga_ref, *, n_tiles):
    i = pl.program_id(1)
    tile = n_tiles - 1 - i
    T = x_ref.shape[1]

    @pl.when(i == 0)
    def _():
        state_ref[...] = jnp.zeros_like(state_ref)

    shift = mod_ref[0, 0:1, :]
    scale1 = 1.0 + mod_ref[0, 1:2, :]
    ng = ng_ref[...]

    def normed(xf):
        return ((xf * _rms_scale(xf) * ng) * scale1 + shift).astype(BF16)

    hb = normed(x_ref[0])
    hb_ref[0] = hb
    zero_halo = jnp.zeros((HALO, D), BF16)
    hprev = jnp.where(tile > 0, normed(xp_ref[0]), zero_halo)
    hnext = jnp.where(tile < n_tiles - 1, normed(xn_ref[0]), zero_halo)
    hx = jnp.concatenate([hprev, hb, hnext], axis=0)

    lr = _dot(hb, wlr_ref[...]).astype(BF16)
    qk = _dot(hb, _wb(wqk_ref[...]))
    z = _dot(lr, wg_ref[...]) + bg_ref[...]
    g2 = (jnp.minimum(z, 0.0) - jnp.log1p(jnp.exp(-jnp.abs(z)))) * (LOG2E / GATE_TEMP)

    q = qk[:, :DK] * (HK ** -0.5)
    k = qk[:, DK:2 * DK]
    vb = _dot(hb, _wb(wv_ref[...])).astype(BF16)
    v_ref[0] = vb
    vt = jnp.transpose(vb)

    r = lax.broadcasted_iota(jnp.int32, (CHUNK, 2 * CHUNK), 0)
    c = lax.broadcasted_iota(jnp.int32, (CHUNK, 2 * CHUNK), 1) % CHUNK
    tri_incl = (c <= r).astype(BF16)
    tri_strict = (c < r).astype(BF16)
    mid = CHUNK // 2
    hilo_of_chunk = {}
    staged = {}

    def pair_cols(hp):
        return slice(hp * 2 * HK, (hp + 1) * 2 * HK), slice(DK + hp * 2 * HK, DK + (hp + 1) * 2 * HK)

    def gla_cumsum(ci, hp):
        rows = slice(ci * CHUNK, (ci + 1) * CHUNK)
        if ci not in hilo_of_chunk:
            gc = g2[rows]
            hi = gc.astype(BF16)
            lo = (gc - hi.astype(F32)).astype(BF16)
            hilo_of_chunk[ci] = (jnp.concatenate([hi, lo], axis=0), gc)
        hilo, _ = hilo_of_chunk[ci]
        pc, pcb = pair_cols(hp)
        staged[ci, hp, 'cum'] = (_dot(tri_incl, hilo[:, pc]), _dot(tri_strict, hilo[:, pcb]))

    def gla_operands(ci, hp):
        rows = slice(ci * CHUNK, (ci + 1) * CHUNK)
        pc, pcb = pair_cols(hp)
        gc = hilo_of_chunk[ci][1]
        gf, eb = staged.pop((ci, hp, 'cum'))
        qc, kc = q[rows, pc], k[rows, pc]
        gm = gf[mid - 1:mid, :]
        gl = gf[CHUNK - 1:CHUNK, :]
        ef = jnp.exp2(lax.clamp(-EXP2_CLAMP, gf - gm, EXP2_CLAMP))
        qaf = (qc * ef).astype(BF16)
        kaf = kc * (1.0 / ef)
        kft_ref[0, ci, pc, :] = jnp.transpose(kaf.astype(BF16))
        em = eb[mid:mid + 1, :]
        et = eb[CHUNK - 1:CHUNK, :] + gc[CHUNK - 1:CHUNK, pcb]
        ebx = jnp.exp2(lax.clamp(-EXP2_CLAMP, em - eb, EXP2_CLAMP))
        qab = (qc * ebx).astype(BF16)
        kab = kc * (1.0 / ebx)
        kbt_ref[0, ci, pc, :] = jnp.transpose(kab.astype(BF16))
        cvec_ref[0, ci, :, pc] = jnp.concatenate(
            [jnp.exp2(gl), jnp.exp2(gm), jnp.zeros((6, 2 * HK), F32)], axis=0)
        for hh in range(2):
            h = 2 * hp + hh
            lc = slice(hh * HK, (hh + 1) * HK)
            qfb_ref[0, rows, 2 * h * HK:(2 * h + 1) * HK] = qaf[:, lc]
            qfb_ref[0, rows, (2 * h + 1) * HK:(2 * h + 2) * HK] = qab[:, lc]
        staged[ci, hp, 'ks'] = ((kaf * jnp.exp2(gl - gm)).astype(BF16), (kab * jnp.exp2(em)).astype(BF16),
                                jnp.exp2(et - em), jnp.exp2(et))

    def gla_state(ci, hp):
        rows = slice(ci * CHUNK, (ci + 1) * CHUNK)
        ksf, ksb, mu_b, gamma_b = staged.pop((ci, hp, 'ks'))
        for hh in range(2):
            h = 2 * hp + hh
            lc = slice(hh * HK, (hh + 1) * HK)
            vc = slice(h * HV, (h + 1) * HV)
            u = _dot(vt[vc, rows], jnp.concatenate([ksf[:, lc], ksb[:, lc]], axis=1))
            uft_ref[0, ci, vc, :] = u[:, :HK].astype(BF16)
            st = state_ref[h]
            sbs_ref[0, ci, h * HK:(h + 1) * HK, :] = jnp.transpose(st * mu_b[:, lc]).astype(BF16)
            state_ref[h] = st * gamma_b[:, lc] + u[:, HK:]

    inner = slice(HALO, T + HALO)
    slab_vals = {}

    def conv_piece(s, part):
        cols = slice(s * SLAB, (s + 1) * SLAB)
        if part == 0:
            slab_vals[s, 'c'] = _dot(hx, _wb(wac_ref[:, cols]))
        elif part == 1:
            p = slab_vals.pop((s, 'c')) * _dot(hx, _wb(wax_ref[:, cols]))
            p_prev = pltpu.roll(p, 1, axis=0)[inner]
            p_next = pltpu.roll(p, T + 2 * HALO - 1, axis=0)[inner]
            slab_vals[s, 'u'] = (cw_ref[0:1, cols] * p_prev + cw_ref[1:2, cols] * p[inner]
                                 + cw_ref[2:3, cols] * p_next + cb_ref[:, cols])
        elif part == 2:
            slab_vals[s, 'bu'] = _dot(hb, _wb(wab_ref[:, cols])) * slab_vals.pop((s, 'u'))
        else:
            az = _dot(hb, _wb(waz_ref[:, cols]))
            ga_ref[:, cols] = (slab_vals.pop((s, 'bu')) * _silu(az)).astype(BF16)

    gla_pieces = [(ci, hp) for ci in reversed(range(T // CHUNK)) for hp in range(HEADS // 2)]
    conv_pieces = [(s, part) for s in range(D // SLAB) for part in range(4)]
    assert len(conv_pieces) == 2 * len(gla_pieces)
    gla_cumsum(*gla_pieces[0])
    for j, gp in enumerate(gla_pieces):
        conv_piece(*conv_pieces[2 * j])
        gla_operands(*gp)
        if j + 1 < len(gla_pieces):
            gla_cumsum(*gla_pieces[j + 1])
        conv_piece(*conv_pieces[2 * j + 1])
        gla_state(*gp)

    for hc in range(2):
        cols = slice(hc * D // 2, (hc + 1) * D // 2)
        m_a = _dot(hb, _wb(wma_ref[:, cols]))
        y_a = _dot(ga_ref[...], _wb(wao_ref[:, cols]))
        yam_ref[0, :, cols] = (_sigmoid(m_a) * y_a).astype(BF16)


def _pre(x, mod3, ng, wp, wlr, wma, cw, cb, wg, bg, wao):
    B, L, _ = x.shape
    T = TILE
    nt = L // T
    nc = T // CHUNK
    tok = lambda w: pl.BlockSpec((1, T, w), lambda b, i: (b, nt - 1 - i, 0))
    hpt = T // HALO
    wcol = lambda j: pl.BlockSpec((D // 2, D), lambda b, i: (0, j), pipeline_mode=pl.Buffered(1))
    in_specs = [
        tok(D),
        pl.BlockSpec((1, HALO, D), lambda b, i: (b, jnp.maximum((nt - 1 - i) * hpt - 1, 0), 0)),
        pl.BlockSpec((1, HALO, D), lambda b, i: (b, jnp.minimum((nt - i) * hpt, L // HALO - 1), 0)),
        pl.BlockSpec((1, 3, D), lambda b, i: (b, 0, 0)),
        _const_spec((1, D)),
        wcol(0), wcol(1), wcol(2), wcol(3),
        wcol(4), wcol(5),
        _const_spec((D, 128)),
        _const_spec((D // 2, D)),
        _const_spec((3, D)),
        _const_spec((1, D)),
        _const_spec((128, 2 * DK)),
        _const_spec((1, 2 * DK)),
        _const_spec((D // 2, D)),
    ]
    out_shape = [
        jax.ShapeDtypeStruct((B, L, 2 * DK), BF16),
        jax.ShapeDtypeStruct((B, L // CHUNK, DK, CHUNK), BF16),
        jax.ShapeDtypeStruct((B, L // CHUNK, DK, CHUNK), BF16),
        jax.ShapeDtypeStruct((B, L, D), BF16),
        jax.ShapeDtypeStruct((B, L, D), BF16),
        jax.ShapeDtypeStruct((B, L, D), BF16),
        jax.ShapeDtypeStruct((B, L // CHUNK, DK, HV), BF16),
        jax.ShapeDtypeStruct((B, L // CHUNK, HEADS * HV, HK), BF16),
        jax.ShapeDtypeStruct((B, L // CHUNK, 8, DK), F32),
    ]
    st_spec = pl.BlockSpec((1, nc, HEADS * HV, HK), lambda b, i: (b, nt - 1 - i, 0, 0))
    kt_spec = pl.BlockSpec((1, nc, DK, CHUNK), lambda b, i: (b, nt - 1 - i, 0, 0))
    sb_spec = pl.BlockSpec((1, nc, DK, HV), lambda b, i: (b, nt - 1 - i, 0, 0))
    out_specs = [tok(2 * DK), kt_spec, kt_spec, tok(D), tok(D), tok(D), sb_spec, st_spec,
                 pl.BlockSpec((1, nc, 8, DK), lambda b, i: (b, nt - 1 - i, 0, 0))]
    return pl.pallas_call(
        functools.partial(_pre_kernel, n_tiles=nt),
        grid=(B, nt),
        in_specs=in_specs,
        out_specs=out_specs,
        out_shape=out_shape,
        scratch_shapes=[pltpu.VMEM((HEADS, HV, HK), F32), pltpu.VMEM((T, D), BF16)],
        compiler_params=pltpu.CompilerParams(
            dimension_semantics=("arbitrary", "arbitrary"), vmem_limit_bytes=VMEM_LIMIT),
        name="encoder_pre",
    )(x, x, x, mod3, ng, wp, wp, wp, wp, wp, wp, wlr, wma, cw, cb, wg, bg, wao)


def _post_kernel(x_ref, qfb_ref, kft_ref, kbt_ref, v_ref, hb_ref, yam_ref,
                 sbt_ref, uft_ref, cvec_ref, mod_ref, gln_ref, wbz_ref, wmb_ref, wbo_ref, wo_ref, fng_ref,
                 o_ref, state_ref, yin_ref):
    i = pl.program_id(1)
    T = x_ref.shape[1]
    n_chunks = T // CHUNK

    @pl.when(i == 0)
    def _():
        state_ref[...] = jnp.zeros_like(state_ref)

    r = lax.broadcasted_iota(jnp.int32, (CHUNK, CHUNK), 0)
    c = lax.broadcasted_iota(jnp.int32, (CHUNK, CHUNK), 1)
    causal = c <= r
    zeros = jnp.zeros((CHUNK, HK), BF16)

    att = {}
    for ci in range(n_chunks):
        rows = slice(ci * CHUNK, (ci + 1) * CHUNK)
        for h in range(HEADS):
            kc = slice(h * HK, (h + 1) * HK)
            kblk = jnp.concatenate(
                [jnp.concatenate([kft_ref[0, ci, kc, :], zeros], axis=1),
                 jnp.concatenate([zeros, kbt_ref[0, ci, kc, :]], axis=1)], axis=0)
            a2 = _dot(qfb_ref[0, rows, 2 * h * HK:(2 * h + 2) * HK], kblk)
            att[ci, h] = jnp.where(causal, a2[:, :CHUNK], a2[:, CHUNK:]).astype(BF16)

    sb = _silu(_dot(hb_ref[0], _wb(wbz_ref[...])))

    scat = {}
    for h in range(HEADS):
        kc = slice(h * HK, (h + 1) * HK)
        vc = slice(h * HV, (h + 1) * HV)
        st = state_ref[h]
        for ci in range(n_chunks):
            cv = cvec_ref[0, ci]
            scat[ci, h] = jnp.concatenate(
                [jnp.transpose(st * cv[1:2, kc]).astype(BF16), sbt_ref[0, ci, kc, :]], axis=0)
            st = st * cv[0:1, kc] + uft_ref[0, ci, vc, :].astype(F32)
        state_ref[h] = st

    for ci in range(n_chunks):
        rows = slice(ci * CHUNK, (ci + 1) * CHUNK)
        for h in range(HEADS):
            vc = slice(h * HV, (h + 1) * HV)
            o = (_dot(att[ci, h], v_ref[0, rows, vc])
                 + _dot(qfb_ref[0, rows, 2 * h * HK:(2 * h + 2) * HK], scat[ci, h]))
            o = o * _rms_scale(o) * gln_ref[:, vc]
            yin_ref[rows, vc] = (o * sb[rows, vc]).astype(BF16)

    smb = _sigmoid(_dot(hb_ref[0], _wb(wmb_ref[...])))
    half = T // 2
    merged = {}
    for hf in range(2):
        rows = slice(hf * half, (hf + 1) * half)
        y_b = _dot(yin_ref[rows, :], _wb(wbo_ref[...]))
        merged[hf] = (yam_ref[0, rows, :].astype(F32) + smb[rows] * y_b).astype(BF16)
    for hf in range(2):
        rows = slice(hf * half, (hf + 1) * half)
        y = x_ref[0, rows, :] + mod_ref[0, 2:3, :] * _dot(merged[hf], _wb(wo_ref[...]))
        o_ref[0, rows, :] = y * _rms_scale(y) * fng_ref[...]


def _post(x, qfb, kft, kbt, v, hb, yam, sbt, uft, cvec, mod3, gln, wbz, wmb, wbo, wo, fng):
    B, L, _ = x.shape
    T = TILE
    nt = L // T
    nc = T // CHUNK
    tok = lambda w: pl.BlockSpec((1, T, w), lambda b, i: (b, i, 0))
    st_spec = pl.BlockSpec((1, nc, HEADS * HV, HK), lambda b, i: (b, i, 0, 0))
    kt_spec = pl.BlockSpec((1, nc, DK, CHUNK), lambda b, i: (b, i, 0, 0))
    sb_spec = pl.BlockSpec((1, nc, DK, HV), lambda b, i: (b, i, 0, 0))
    in_specs = [tok(D), tok(2 * DK), kt_spec, kt_spec, tok(D), tok(D), tok(D),
                sb_spec, st_spec,
                pl.BlockSpec((1, nc, 8, DK), lambda b, i: (b, i, 0, 0)),
                pl.BlockSpec((1, 3, D), lambda b, i: (b, 0, 0)),
                _const_spec((1, D)), _const_spec((D // 2, D)), _const_spec((D // 2, D)),
                _const_spec((D // 2, D)), _const_spec((D // 2, D)), _const_spec((1, D))]
    return pl.pallas_call(
        _post_kernel,
        grid=(B, nt),
        in_specs=in_specs,
        out_specs=tok(D),
        out_shape=jax.ShapeDtypeStruct((B, L, D), F32),
        scratch_shapes=[pltpu.VMEM((HEADS, HV, HK), F32), pltpu.VMEM((T, D), BF16)],
        compiler_params=pltpu.CompilerParams(
            dimension_semantics=("arbitrary", "arbitrary"), vmem_limit_bytes=VMEM_LIMIT),
        name="encoder_post",
    )(x, qfb, kft, kbt, v, hb, yam, sbt, uft, cvec, mod3, gln, wbz, wmb, wbo, wo, fng)


def _pack_row_pairs(w32):
    bits = lax.bitcast_convert_type(w32, jnp.uint32)
    bits = bits + (jnp.uint32(0x7FFF) + ((bits >> 16) & jnp.uint32(1)))
    return (bits[0::2] >> 16) | (bits[1::2] & jnp.uint32(0xFFFF0000))


def kernel(x_prompt, x_sample, c_prompt, c_sample, w_ada, b_ada, norm_g, w_in, conv_w, conv_b,
           w_gate_f, b_gate_f, w_gate_b, b_gate_b, gla_norm_g, w_a_out, w_b_out, w_out,
           final_norm_g):
    assert w_ada.shape[0] == 1, "one layer"
    o_q = 4 * D
    o_lr = o_q + 2 * DK + 2 * D
    o_m = o_lr + 2 * RANK
    wp = _pack_row_pairs(w_in[0])
    wlr = jnp.pad(w_in[0][:, o_lr:o_m].astype(BF16), ((0, 0), (0, 128 - 2 * RANK)))
    wbz = wp[:, o_q + 2 * DK + D:o_lr]
    wma = wp[:, o_m:o_m + D]
    wmb = wp[:, o_m + D:]
    wg = jnp.zeros((128, 2 * DK), F32)
    wg = wg.at[:RANK, :DK].set(w_gate_f[0]).at[RANK:2 * RANK, DK:].set(w_gate_b[0]).astype(BF16)
    bg = jnp.concatenate([b_gate_f[0], b_gate_b[0]])[None, :]
    ng = norm_g[0][None, :]
    cw = conv_w[0]
    cb = conv_b[0][None, :]
    gln = gla_norm_g[0][None, :]
    wao = _pack_row_pairs(w_a_out[0])
    wbo = _pack_row_pairs(w_b_out[0])
    wo = _pack_row_pairs(w_out[0])
    fng = final_norm_g[None, :]

    nb = x_prompt.shape[0]
    mod = _modulation(jnp.concatenate([c_prompt, c_sample], axis=0), w_ada[0], b_ada[0][None, :])
    mod3 = mod.reshape(mod.shape[0], 3, D)

    def encode(x, m3):
        pre = _pre(x, m3, ng, wp, wlr, wma, cw, cb, wg, bg, wao)
        return _post(x, *pre, m3, gln, wbz, wmb, wbo, wo, fng)

    return (encode(x_prompt, mod3[:nb]), encode(x_sample, mod3[nb:]))
```

```python
import functools

import jax
import jax.numpy as jnp
from jax import lax
from jax.experimental import pallas as pl
from jax.experimental.pallas import tpu as pltpu

D = 1024
HEADS = 4
DK = 512
HK = DK // HEADS
HV = D // HEADS
RANK = 16
GATE_TEMP = 16.0
EPS = 1e-6
CHUNK = 128
TILE = 512
HALO = 16
SLAB = 256
LOG2E = 1.4426950408889634
EXP2_CLAMP = 115.0
VMEM_LIMIT = 58 * 1024 * 1024

F32 = jnp.float32
BF16 = jnp.bfloat16


def _dot(a, b):
    return jnp.dot(a, b, preferred_element_type=F32)


def _sigmoid(x):
    return 0.5 + 0.5 * jnp.tanh(0.5 * x)


def _silu(x):
    h = 0.5 * x
    return h + h * jnp.tanh(h)


def _rms_scale(x):
    return lax.rsqrt(jnp.mean(x * x, axis=-1, keepdims=True) + EPS)


def _const_spec(shape):
    zeros = (0,) * len(shape)
    return pl.BlockSpec(shape, lambda b, i: zeros, pipeline_mode=pl.Buffered(1))


def _mod_kernel(c_ref, w_ref, b_ref, o_ref):
    o_ref[...] = jnp.dot(_silu(c_ref[...]), w_ref[...], preferred_element_type=F32,
                         precision=lax.Precision.HIGHEST) + b_ref[...]


def _modulation(c, w_ada, b_ada):
    n = c.shape[0]
    return pl.pallas_call(
        _mod_kernel,
        out_shape=jax.ShapeDtypeStruct((n, 3 * D), F32),
        compiler_params=pltpu.CompilerParams(vmem_limit_bytes=VMEM_LIMIT),
        name="adaln_mod",
    )(c, w_ada, b_ada)


def _pre_kernel(x_ref, xp_ref, xn_ref, mod_ref, ng_ref, wab_ref, wac_ref, wax_ref, waz_ref,
                wqk_ref, wv_ref, wlr_ref, wma_ref,
                cw_ref, cb_ref, wg_ref, bg_ref, wao_ref,
                qfb_ref, kft_ref, kbt_ref, v_ref, hb_ref, yam_ref,
                sbs_ref, uft_ref, cvec_ref, state_ref, ga_ref, wn_ref, *, n_tiles):
    i = pl.program_id(1)
    tile = n_tiles - 1 - i
    T = x_ref.shape[1]

    @pl.when(i == 0)
    def _():
        state_ref[...] = jnp.zeros_like(state_ref)

    @pl.when((i == 0) & (pl.program_id(0) == 0))
    def _():
        for j, wr in enumerate((wab_ref, wac_ref, wax_ref, waz_ref, wqk_ref, wv_ref)):
            wn_ref[j] = wr[...]

    shift = mod_ref[0, 0:1, :]
    scale1 = 1.0 + mod_ref[0, 1:2, :]
    ng = ng_ref[...]

    def normed(xf):
        return ((xf * _rms_scale(xf) * ng) * scale1 + shift).astype(BF16)

    hb = normed(x_ref[0])
    hb_ref[0] = hb
    zero_halo = jnp.zeros((HALO, D), BF16)
    hprev = jnp.where(tile > 0, normed(xp_ref[0]), zero_halo)
    hnext = jnp.where(tile < n_tiles - 1, normed(xn_ref[0]), zero_halo)
    hx = jnp.concatenate([hprev, hb, hnext], axis=0)

    lr = _dot(hb, wlr_ref[...]).astype(BF16)
    qk = _dot(hb, wn_ref[4])
    z = _dot(lr, wg_ref[...]) + bg_ref[...]
    g2 = (jnp.minimum(z, 0.0) - jnp.log1p(jnp.exp(-jnp.abs(z)))) * (LOG2E / GATE_TEMP)

    q = qk[:, :DK] * (HK ** -0.5)
    k = qk[:, DK:2 * DK]
    vb = _dot(hb, wn_ref[5]).astype(BF16)
    v_ref[0] = vb
    vt = jnp.transpose(vb)

    r = lax.broadcasted_iota(jnp.int32, (CHUNK, 2 * CHUNK), 0)
    c = lax.broadcasted_iota(jnp.int32, (CHUNK, 2 * CHUNK), 1) % CHUNK
    tri_incl = (c <= r).astype(BF16)
    tri_strict = (c < r).astype(BF16)
    mid = CHUNK // 2
    hilo_of_chunk = {}
    staged = {}

    def pair_cols(hp):
        return slice(hp * 2 * HK, (hp + 1) * 2 * HK), slice(DK + hp * 2 * HK, DK + (hp + 1) * 2 * HK)

    def gla_cumsum(ci, hp):
        rows = slice(ci * CHUNK, (ci + 1) * CHUNK)
        if ci not in hilo_of_chunk:
            gc = g2[rows]
            hi = gc.astype(BF16)
            lo = (gc - hi.astype(F32)).astype(BF16)
            hilo_of_chunk[ci] = (jnp.concatenate([hi, lo], axis=0), gc)
        hilo, _ = hilo_of_chunk[ci]
        pc, pcb = pair_cols(hp)
        staged[ci, hp, 'cum'] = (_dot(tri_incl, hilo[:, pc]), _dot(tri_strict, hilo[:, pcb]))

    def gla_operands(ci, hp):
        rows = slice(ci * CHUNK, (ci + 1) * CHUNK)
        pc, pcb = pair_cols(hp)
        gc = hilo_of_chunk[ci][1]
        gf, eb = staged.pop((ci, hp, 'cum'))
        qc, kc = q[rows, pc], k[rows, pc]
        gm = gf[mid - 1:mid, :]
        gl = gf[CHUNK - 1:CHUNK, :]
        ef = jnp.exp2(lax.clamp(-EXP2_CLAMP, gf - gm, EXP2_CLAMP))
        qaf = (qc * ef).astype(BF16)
        kaf = kc * (1.0 / ef)
        kft_ref[0, ci, pc, :] = jnp.transpose(kaf.astype(BF16))
        em = eb[mid:mid + 1, :]
        et = eb[CHUNK - 1:CHUNK, :] + gc[CHUNK - 1:CHUNK, pcb]
        ebx = jnp.exp2(lax.clamp(-EXP2_CLAMP, em - eb, EXP2_CLAMP))
        qab = (qc * ebx).astype(BF16)
        kab = kc * (1.0 / ebx)
        kbt_ref[0, ci, pc, :] = jnp.transpose(kab.astype(BF16))
        cvec_ref[0, ci, :, pc] = jnp.concatenate(
            [jnp.exp2(gl), jnp.exp2(gm), jnp.zeros((6, 2 * HK), F32)], axis=0)
        for hh in range(2):
            h = 2 * hp + hh
            lc = slice(hh * HK, (hh + 1) * HK)
            qfb_ref[0, rows, 2 * h * HK:(2 * h + 1) * HK] = qaf[:, lc]
            qfb_ref[0, rows, (2 * h + 1) * HK:(2 * h + 2) * HK] = qab[:, lc]
        staged[ci, hp, 'ks'] = ((kaf * jnp.exp2(gl - gm)).astype(BF16), (kab * jnp.exp2(em)).astype(BF16),
                                jnp.exp2(et - em), jnp.exp2(et))

    def gla_state(ci, hp):
        rows = slice(ci * CHUNK, (ci + 1) * CHUNK)
        ksf, ksb, mu_b, gamma_b = staged.pop((ci, hp, 'ks'))
        for hh in range(2):
            h = 2 * hp + hh
            lc = slice(hh * HK, (hh + 1) * HK)
            vc = slice(h * HV, (h + 1) * HV)
            u = _dot(vt[vc, rows], jnp.concatenate([ksf[:, lc], ksb[:, lc]], axis=1))
            uft_ref[0, ci, vc, :] = u[:, :HK].astype(BF16)
            st = state_ref[h]
            sbs_ref[0, ci, h * HK:(h + 1) * HK, :] = jnp.transpose(st * mu_b[:, lc]).astype(BF16)
            state_ref[h] = st * gamma_b[:, lc] + u[:, HK:]

    inner = slice(HALO, T + HALO)
    slab_vals = {}

    def conv_piece(s, part):
        cols = slice(s * SLAB, (s + 1) * SLAB)
        if part == 0:
            slab_vals[s, 'c'] = _dot(hx, wn_ref[1, :, cols])
        elif part == 1:
            p = slab_vals.pop((s, 'c')) * _dot(hx, wn_ref[2, :, cols])
            p_prev = pltpu.roll(p, 1, axis=0)[inner]
            p_next = pltpu.roll(p, T + 2 * HALO - 1, axis=0)[inner]
            slab_vals[s, 'u'] = (cw_ref[0:1, cols] * p_prev + cw_ref[1:2, cols] * p[inner]
                                 + cw_ref[2:3, cols] * p_next + cb_ref[:, cols])
        elif part == 2:
            slab_vals[s, 'bu'] = _dot(hb, wn_ref[0, :, cols]) * slab_vals.pop((s, 'u'))
        else:
            az = _dot(hb, wn_ref[3, :, cols])
            ga_ref[:, cols] = (slab_vals.pop((s, 'bu')) * _silu(az)).astype(BF16)

    gla_pieces = [(ci, hp) for ci in reversed(range(T // CHUNK)) for hp in range(HEADS // 2)]
    conv_pieces = [(s, part) for s in range(D // SLAB) for part in range(4)]
    assert len(conv_pieces) == 2 * len(gla_pieces)
    gla_cumsum(*gla_pieces[0])
    for j, gp in enumerate(gla_pieces):
        conv_piece(*conv_pieces[2 * j])
        gla_operands(*gp)
        if j + 1 < len(gla_pieces):
            gla_cumsum(*gla_pieces[j + 1])
        conv_piece(*conv_pieces[2 * j + 1])
        gla_state(*gp)

    for hc in range(2):
        cols = slice(hc * D // 2, (hc + 1) * D // 2)
        m_a = _dot(hb, wma_ref[:, cols])
        y_a = _dot(ga_ref[...], wao_ref[:, cols])
        yam_ref[0, :, cols] = (_sigmoid(m_a) * y_a).astype(BF16)


def _pre(x, mod3, ng, w, wlr, wma, cw, cb, wg, bg, wao):
    B, L, _ = x.shape
    T = TILE
    nt = L // T
    nc = T // CHUNK
    tok = lambda w: pl.BlockSpec((1, T, w), lambda b, i: (b, nt - 1 - i, 0))
    hpt = T // HALO
    wcol = lambda j: pl.BlockSpec((D, D), lambda b, i: (0, j), pipeline_mode=pl.Buffered(1))
    in_specs = [
        tok(D),
        pl.BlockSpec((1, HALO, D), lambda b, i: (b, jnp.maximum((nt - 1 - i) * hpt - 1, 0), 0)),
        pl.BlockSpec((1, HALO, D), lambda b, i: (b, jnp.minimum((nt - i) * hpt, L // HALO - 1), 0)),
        pl.BlockSpec((1, 3, D), lambda b, i: (b, 0, 0)),
        _const_spec((1, D)),
        wcol(0), wcol(1), wcol(2), wcol(3),
        wcol(4), wcol(5),
        _const_spec((D, 128)),
        _const_spec((D, D)),
        _const_spec((3, D)),
        _const_spec((1, D)),
        _const_spec((128, 2 * DK)),
        _const_spec((1, 2 * DK)),
        _const_spec((D, D)),
    ]
    out_shape = [
        jax.ShapeDtypeStruct((B, L, 2 * DK), BF16),
        jax.ShapeDtypeStruct((B, L // CHUNK, DK, CHUNK), BF16),
        jax.ShapeDtypeStruct((B, L // CHUNK, DK, CHUNK), BF16),
        jax.ShapeDtypeStruct((B, L, D), BF16),
        jax.ShapeDtypeStruct((B, L, D), BF16),
        jax.ShapeDtypeStruct((B, L, D), BF16),
        jax.ShapeDtypeStruct((B, L // CHUNK, DK, HV), BF16),
        jax.ShapeDtypeStruct((B, L // CHUNK, HEADS * HV, HK), BF16),
        jax.ShapeDtypeStruct((B, L // CHUNK, 8, DK), F32),
    ]
    st_spec = pl.BlockSpec((1, nc, HEADS * HV, HK), lambda b, i: (b, nt - 1 - i, 0, 0))
    kt_spec = pl.BlockSpec((1, nc, DK, CHUNK), lambda b, i: (b, nt - 1 - i, 0, 0))
    sb_spec = pl.BlockSpec((1, nc, DK, HV), lambda b, i: (b, nt - 1 - i, 0, 0))
    out_specs = [tok(2 * DK), kt_spec, kt_spec, tok(D), tok(D), tok(D), sb_spec, st_spec,
                 pl.BlockSpec((1, nc, 8, DK), lambda b, i: (b, nt - 1 - i, 0, 0))]
    return pl.pallas_call(
        functools.partial(_pre_kernel, n_tiles=nt),
        grid=(B, nt),
        in_specs=in_specs,
        out_specs=out_specs,
        out_shape=out_shape,
        scratch_shapes=[pltpu.VMEM((HEADS, HV, HK), F32), pltpu.VMEM((T, D), BF16),
                        pltpu.VMEM((6, D, D), BF16)],
        compiler_params=pltpu.CompilerParams(
            dimension_semantics=("arbitrary", "arbitrary"), vmem_limit_bytes=VMEM_LIMIT),
        name="encoder_pre",
    )(x, x, x, mod3, ng, w, w, w, w, w, w, wlr, wma, cw, cb, wg, bg, wao)


def _post_kernel(x_ref, qfb_ref, kft_ref, kbt_ref, v_ref, hb_ref, yam_ref,
                 sbt_ref, uft_ref, cvec_ref, mod_ref, gln_ref, wbz_ref, wmb_ref, wbo_ref, wo_ref, fng_ref,
                 o_ref, state_ref, yin_ref, wn_ref):
    i = pl.program_id(1)
    T = x_ref.shape[1]
    n_chunks = T // CHUNK

    @pl.when(i == 0)
    def _():
        state_ref[...] = jnp.zeros_like(state_ref)

    @pl.when((i == 0) & (pl.program_id(0) == 0))
    def _():
        for j, wr in enumerate((wbz_ref, wmb_ref, wbo_ref, wo_ref)):
            wn_ref[j] = wr[...]

    r = lax.broadcasted_iota(jnp.int32, (CHUNK, CHUNK), 0)
    c = lax.broadcasted_iota(jnp.int32, (CHUNK, CHUNK), 1)
    causal = c <= r
    zeros = jnp.zeros((CHUNK, HK), BF16)

    att = {}
    for ci in range(n_chunks):
        rows = slice(ci * CHUNK, (ci + 1) * CHUNK)
        for h in range(HEADS):
            kc = slice(h * HK, (h + 1) * HK)
            kblk = jnp.concatenate(
                [jnp.concatenate([kft_ref[0, ci, kc, :], zeros], axis=1),
                 jnp.concatenate([zeros, kbt_ref[0, ci, kc, :]], axis=1)], axis=0)
            a2 = _dot(qfb_ref[0, rows, 2 * h * HK:(2 * h + 2) * HK], kblk)
            att[ci, h] = jnp.where(causal, a2[:, :CHUNK], a2[:, CHUNK:]).astype(BF16)

    sb = _silu(_dot(hb_ref[0], wn_ref[0]))

    scat = {}
    for h in range(HEADS):
        kc = slice(h * HK, (h + 1) * HK)
        vc = slice(h * HV, (h + 1) * HV)
        st = state_ref[h]
        for ci in range(n_chunks):
            cv = cvec_ref[0, ci]
            scat[ci, h] = jnp.concatenate(
                [jnp.transpose(st * cv[1:2, kc]).astype(BF16), sbt_ref[0, ci, kc, :]], axis=0)
            st = st * cv[0:1, kc] + uft_ref[0, ci, vc, :].astype(F32)
        state_ref[h] = st

    for ci in range(n_chunks):
        rows = slice(ci * CHUNK, (ci + 1) * CHUNK)
        for h in range(HEADS):
            vc = slice(h * HV, (h + 1) * HV)
            o = (_dot(att[ci, h], v_ref[0, rows, vc])
                 + _dot(qfb_ref[0, rows, 2 * h * HK:(2 * h + 2) * HK], scat[ci, h]))
            o = o * _rms_scale(o) * gln_ref[:, vc]
            yin_ref[rows, vc] = (o * sb[rows, vc]).astype(BF16)

    smb = _sigmoid(_dot(hb_ref[0], wn_ref[1]))
    half = T // 2
    merged = {}
    for hf in range(2):
        rows = slice(hf * half, (hf + 1) * half)
        y_b = _dot(yin_ref[rows, :], wn_ref[2])
        merged[hf] = (yam_ref[0, rows, :].astype(F32) + smb[rows] * y_b).astype(BF16)
    for hf in range(2):
        rows = slice(hf * half, (hf + 1) * half)
        y = x_ref[0, rows, :] + mod_ref[0, 2:3, :] * _dot(merged[hf], wn_ref[3])
        o_ref[0, rows, :] = y * _rms_scale(y) * fng_ref[...]


def _post(x, qfb, kft, kbt, v, hb, yam, sbt, uft, cvec, mod3, gln, wbz, wmb, wbo, wo, fng):
    B, L, _ = x.shape
    T = TILE
    nt = L // T
    nc = T // CHUNK
    tok = lambda w: pl.BlockSpec((1, T, w), lambda b, i: (b, i, 0))
    st_spec = pl.BlockSpec((1, nc, HEADS * HV, HK), lambda b, i: (b, i, 0, 0))
    kt_spec = pl.BlockSpec((1, nc, DK, CHUNK), lambda b, i: (b, i, 0, 0))
    sb_spec = pl.BlockSpec((1, nc, DK, HV), lambda b, i: (b, i, 0, 0))
    in_specs = [tok(D), tok(2 * DK), kt_spec, kt_spec, tok(D), tok(D), tok(D),
                sb_spec, st_spec,
                pl.BlockSpec((1, nc, 8, DK), lambda b, i: (b, i, 0, 0)),
                pl.BlockSpec((1, 3, D), lambda b, i: (b, 0, 0)),
                _const_spec((1, D)), _const_spec((D, D)), _const_spec((D, D)), _const_spec((D, D)),
                _const_spec((D, D)), _const_spec((1, D))]
    return pl.pallas_call(
        _post_kernel,
        grid=(B, nt),
        in_specs=in_specs,
        out_specs=tok(D),
        out_shape=jax.ShapeDtypeStruct((B, L, D), F32),
        scratch_shapes=[pltpu.VMEM((HEADS, HV, HK), F32), pltpu.VMEM((T, D), BF16),
                        pltpu.VMEM((4, D, D), BF16)],
        compiler_params=pltpu.CompilerParams(
            dimension_semantics=("arbitrary", "arbitrary"), vmem_limit_bytes=VMEM_LIMIT),
        name="encoder_post",
    )(x, qfb, kft, kbt, v, hb, yam, sbt, uft, cvec, mod3, gln, wbz, wmb, wbo, wo, fng)


def kernel(x_prompt, x_sample, c_prompt, c_sample, w_ada, b_ada, norm_g, w_in, conv_w, conv_b,
           w_gate_f, b_gate_f, w_gate_b, b_gate_b, gla_norm_g, w_a_out, w_b_out, w_out,
           final_norm_g):
    assert w_ada.shape[0] == 1, "one layer"
    w = w_in[0].astype(BF16)
    o_q = 4 * D
    o_lr = o_q + 2 * DK + 2 * D
    o_m = o_lr + 2 * RANK
    wlr = jnp.pad(w[:, o_lr:o_m], ((0, 0), (0, 128 - 2 * RANK)))
    wbz = w[:, o_q + 2 * DK + D:o_lr]
    wma = w[:, o_m:o_m + D]
    wmb = w[:, o_m + D:]
    wg = jnp.zeros((128, 2 * DK), F32)
    wg = wg.at[:RANK, :DK].set(w_gate_f[0]).at[RANK:2 * RANK, DK:].set(w_gate_b[0]).astype(BF16)
    bg = jnp.concatenate([b_gate_f[0], b_gate_b[0]])[None, :]
    ng = norm_g[0][None, :]
    cw = conv_w[0]
    cb = conv_b[0][None, :]
    gln = gla_norm_g[0][None, :]
    wao = w_a_out[0].astype(BF16)
    wbo = w_b_out[0].astype(BF16)
    wo = w_out[0].astype(BF16)
    fng = final_norm_g[None, :]

    nb = x_prompt.shape[0]
    mod = _modulation(jnp.concatenate([c_prompt, c_sample], axis=0), w_ada[0], b_ada[0][None, :])
    mod3 = mod.reshape(mod.shape[0], 3, D)

    def encode(x, m3):
        pre = _pre(x, m3, ng, w, wlr, wma, cw, cb, wg, bg, wao)
        return _post(x, *pre, m3, gln, wbz, wmb, wbo, wo, fng)

    return (encode(x_prompt, mod3[:nb]), encode(x_sample, mod3[nb:]))
```

```python
import functools

import jax
import jax.numpy as jnp
from jax import lax
from jax.experimental import pallas as pl
from jax.experimental.pallas import tpu as pltpu

D = 1024
HEADS = 4
DK = 512
HK = DK // HEADS
HV = D // HEADS
RANK = 16
GATE_TEMP = 16.0
EPS = 1e-6
CHUNK = 128
TILE = 512
HALO = 16
SLAB = 256
LOG2E = 1.4426950408889634
EXP2_CLAMP = 115.0
VMEM_LIMIT = 58 * 1024 * 1024

F32 = jnp.float32
BF16 = jnp.bfloat16


def _dot(a, b):
    return jnp.dot(a, b, preferred_element_type=F32)


def _wb(w_u32):
    return pltpu.bitcast(w_u32, BF16)


def _sigmoid(x):
    return 0.5 + 0.5 * jnp.tanh(0.5 * x)


def _silu(x):
    h = 0.5 * x
    return h + h * jnp.tanh(h)


def _rms_scale(x):
    return lax.rsqrt(jnp.mean(x * x, axis=-1, keepdims=True) + EPS)


def _const_spec(shape):
    zeros = (0,) * len(shape)
    return pl.BlockSpec(shape, lambda b, i: zeros, pipeline_mode=pl.Buffered(1))


def _mod_kernel(c_ref, w_ref, b_ref, o_ref):
    o_ref[...] = jnp.dot(_silu(c_ref[...]), w_ref[...], preferred_element_type=F32,
                         precision=lax.Precision.HIGHEST) + b_ref[...]


def _modulation(c, w_ada, b_ada):
    n = c.shape[0]
    return pl.pallas_call(
        _mod_kernel,
        out_shape=jax.ShapeDtypeStruct((n, 3 * D), F32),
        compiler_params=pltpu.CompilerParams(vmem_limit_bytes=VMEM_LIMIT),
        name="adaln_mod",
    )(c, w_ada, b_ada)


def _pre_kernel(x_ref, xp_ref, xn_ref, mod_ref, ng_ref, wab_ref, wac_ref, wax_ref, waz_ref,
                wqk_ref, wv_ref, wlr_ref, wma_ref,
                cw_ref, cb_ref, wg_ref, bg_ref, wao_ref,
                qfb_ref, kft_ref, kbt_ref, v_ref, hb_ref, yam_ref,
                sbs_ref, uft_ref, cvec_ref, state_ref, ga_ref, *, n_tiles):
    i = pl.program_id(1)
    tile = n_tiles - 1 - i
    T = x_ref.shape[1]

    @pl.when(i == 0)
    def _():
        state_ref[...] = jnp.zeros_like(state_ref)

    shift = mod_ref[0, 0:1, :]
    scale1 = 1.0 + mod_ref[0, 1:2, :]
    ng = ng_ref[...]

    def normed(xf):
        return ((xf * _rms_scale(xf) * ng) * scale1 + shift).astype(BF16)

    hb = normed(x_ref[0])
    hb_ref[0] = hb
    zero_halo = jnp.zeros((HALO, D), BF16)
    hprev = jnp.where(tile > 0, normed(xp_ref[0]), zero_halo)
    hnext = jnp.where(tile < n_tiles - 1, normed(xn_ref[0]), zero_halo)
    hx = jnp.concatenate([hprev, hb, hnext], axis=0)

    lr = _dot(hb, wlr_ref[...]).astype(BF16)
    qk = _dot(hb, _wb(wqk_ref[...]))
    z = _dot(lr, wg_ref[...]) + bg_ref[...]
    g2 = (jnp.minimum(z, 0.0) - jnp.log1p(jnp.exp(-jnp.abs(z)))) * (LOG2E / GATE_TEMP)

    q = qk[:, :DK] * (HK ** -0.5)
    k = qk[:, DK:2 * DK]
    vb = _dot(hb, _wb(wv_ref[...])).astype(BF16)
    v_ref[0] = vb
    vt = jnp.transpose(vb)

    r = lax.broadcasted_iota(jnp.int32, (CHUNK, 2 * CHUNK), 0)
    c = lax.broadcasted_iota(jnp.int32, (CHUNK, 2 * CHUNK), 1) % CHUNK
    tri_incl = (c <= r).astype(BF16)
    tri_strict = (c < r).astype(BF16)
    mid = CHUNK // 2
    hilo_of_chunk = {}
    staged = {}

    def pair_cols(hp):
        return slice(hp * 2 * HK, (hp + 1) * 2 * HK), slice(DK + hp * 2 * HK, DK + (hp + 1) * 2 * HK)

    def gla_cumsum(ci, hp):
        rows = slice(ci * CHUNK, (ci + 1) * CHUNK)
        if ci not in hilo_of_chunk:
            gc = g2[rows]
            hi = gc.astype(BF16)
            lo = (gc - hi.astype(F32)).astype(BF16)
            hilo_of_chunk[ci] = (jnp.concatenate([hi, lo], axis=0), gc)
        hilo, _ = hilo_of_chunk[ci]
        pc, pcb = pair_cols(hp)
        staged[ci, hp, 'cum'] = (_dot(tri_incl, hilo[:, pc]), _dot(tri_strict, hilo[:, pcb]))

    def gla_operands(ci, hp):
        rows = slice(ci * CHUNK, (ci + 1) * CHUNK)
        pc, pcb = pair_cols(hp)
        gc = hilo_of_chunk[ci][1]
        gf, eb = staged.pop((ci, hp, 'cum'))
        qc, kc = q[rows, pc], k[rows, pc]
        gm = gf[mid - 1:mid, :]
        gl = gf[CHUNK - 1:CHUNK, :]
        ef = jnp.exp2(lax.clamp(-EXP2_CLAMP, gf - gm, EXP2_CLAMP))
        qaf = (qc * ef).astype(BF16)
        kaf = kc * (1.0 / ef)
        kft_ref[0, ci, pc, :] = jnp.transpose(kaf.astype(BF16))
        em = eb[mid:mid + 1, :]
        et = eb[CHUNK - 1:CHUNK, :] + gc[CHUNK - 1:CHUNK, pcb]
        ebx = jnp.exp2(lax.clamp(-EXP2_CLAMP, em - eb, EXP2_CLAMP))
        qab = (qc * ebx).astype(BF16)
        kab = kc * (1.0 / ebx)
        kbt_ref[0, ci, pc, :] = jnp.transpose(kab.astype(BF16))
        cvec_ref[0, ci, :, pc] = jnp.concatenate(
            [jnp.exp2(gl), jnp.exp2(gm), jnp.zeros((6, 2 * HK), F32)], axis=0)
        for hh in range(2):
            h = 2 * hp + hh
            lc = slice(hh * HK, (hh + 1) * HK)
            qfb_ref[0, rows, 2 * h * HK:(2 * h + 1) * HK] = qaf[:, lc]
            qfb_ref[0, rows, (2 * h + 1) * HK:(2 * h + 2) * HK] = qab[:, lc]
        staged[ci, hp, 'ks'] = ((kaf * jnp.exp2(gl - gm)).astype(BF16), (kab * jnp.exp2(em)).astype(BF16),
                                jnp.exp2(et - em), jnp.exp2(et))

    def gla_state(ci, hp):
        rows = slice(ci * CHUNK, (ci + 1) * CHUNK)
        ksf, ksb, mu_b, gamma_b = staged.pop((ci, hp, 'ks'))
        for hh in range(2):
            h = 2 * hp + hh
            lc = slice(hh * HK, (hh + 1) * HK)
            vc = slice(h * HV, (h + 1) * HV)
            u = _dot(vt[vc, rows], jnp.concatenate([ksf[:, lc], ksb[:, lc]], axis=1))
            uft_ref[0, ci, vc, :] = u[:, :HK].astype(BF16)
            st = state_ref[h]
            sbs_ref[0, ci, h * HK:(h + 1) * HK, :] = jnp.transpose(st * mu_b[:, lc]).astype(BF16)
            state_ref[h] = st * gamma_b[:, lc] + u[:, HK:]

    inner = slice(HALO, T + HALO)
    slab_vals = {}

    def conv_piece(s, part):
        cols = slice(s * SLAB, (s + 1) * SLAB)
        if part == 0:
            slab_vals[s, 'c'] = _dot(hx, _wb(wac_ref[:, cols]))
        elif part == 1:
            p = slab_vals.pop((s, 'c')) * _dot(hx, _wb(wax_ref[:, cols]))
            p_prev = pltpu.roll(p, 1, axis=0)[inner]
            p_next = pltpu.roll(p, T + 2 * HALO - 1, axis=0)[inner]
            slab_vals[s, 'u'] = (cw_ref[0:1, cols] * p_prev + cw_ref[1:2, cols] * p[inner]
                                 + cw_ref[2:3, cols] * p_next + cb_ref[:, cols])
        elif part == 2:
            slab_vals[s, 'bu'] = _dot(hb, _wb(wab_ref[:, cols])) * slab_vals.pop((s, 'u'))
        else:
            az = _dot(hb, _wb(waz_ref[:, cols]))
            ga_ref[:, cols] = (slab_vals.pop((s, 'bu')) * _silu(az)).astype(BF16)

    gla_pieces = [(ci, hp) for ci in reversed(range(T // CHUNK)) for hp in range(HEADS // 2)]
    conv_pieces = [(s, part) for s in range(D // SLAB) for part in range(4)]
    assert len(conv_pieces) == 2 * len(gla_pieces)
    gla_cumsum(*gla_pieces[0])
    for j, gp in enumerate(gla_pieces):
        conv_piece(*conv_pieces[2 * j])
        gla_operands(*gp)
        if j + 1 < len(gla_pieces):
            gla_cumsum(*gla_pieces[j + 1])
        conv_piece(*conv_pieces[2 * j + 1])
        gla_state(*gp)

    for hc in range(2):
        cols = slice(hc * D // 2, (hc + 1) * D // 2)
        m_a = _dot(hb, _wb(wma_ref[:, cols]))
        y_a = _dot(ga_ref[...], _wb(wao_ref[:, cols]))
        yam_ref[0, :, cols] = (_sigmoid(m_a) * y_a).astype(BF16)


def _pre(x, mod3, ng, wp, wlr, wma, cw, cb, wg, bg, wao):
    B, L, _ = x.shape
    T = TILE
    nt = L // T
    nc = T // CHUNK
    tok = lambda w: pl.BlockSpec((1, T, w), lambda b, i: (b, nt - 1 - i, 0))
    hpt = T // HALO
    wcol = lambda j: pl.BlockSpec((D // 2, D), lambda b, i: (0, j), pipeline_mode=pl.Buffered(1))
    in_specs = [
        tok(D),
        pl.BlockSpec((1, HALO, D), lambda b, i: (b, jnp.maximum((nt - 1 - i) * hpt - 1, 0), 0)),
        pl.BlockSpec((1, HALO, D), lambda b, i: (b, jnp.minimum((nt - i) * hpt, L // HALO - 1), 0)),
        pl.BlockSpec((1, 3, D), lambda b, i: (b, 0, 0)),
        _const_spec((1, D)),
        wcol(0), wcol(1), wcol(2), wcol(3),
        wcol(4), wcol(5),
        _const_spec((D, 128)),
        _const_spec((D // 2, D)),
        _const_spec((3, D)),
        _const_spec((1, D)),
        _const_spec((128, 2 * DK)),
        _const_spec((1, 2 * DK)),
        _const_spec((D // 2, D)),
    ]
    out_shape = [
        jax.ShapeDtypeStruct((B, L, 2 * DK), BF16),
        jax.ShapeDtypeStruct((B, L // CHUNK, DK, CHUNK), BF16),
        jax.ShapeDtypeStruct((B, L // CHUNK, DK, CHUNK), BF16),
        jax.ShapeDtypeStruct((B, L, D), BF16),
        jax.ShapeDtypeStruct((B, L, D), BF16),
        jax.ShapeDtypeStruct((B, L, D), BF16),
        jax.ShapeDtypeStruct((B, L // CHUNK, DK, HV), BF16),
        jax.ShapeDtypeStruct((B, L // CHUNK, HEADS * HV, HK), BF16),
        jax.ShapeDtypeStruct((B, L // CHUNK, 8, DK), F32),
    ]
    st_spec = pl.BlockSpec((1, nc, HEADS * HV, HK), lambda b, i: (b, nt - 1 - i, 0, 0))
    kt_spec = pl.BlockSpec((1, nc, DK, CHUNK), lambda b, i: (b, nt - 1 - i, 0, 0))
    sb_spec = pl.BlockSpec((1, nc, DK, HV), lambda b, i: (b, nt - 1 - i, 0, 0))
    out_specs = [tok(2 * DK), kt_spec, kt_spec, tok(D), tok(D), tok(D), sb_spec, st_spec,
                 pl.BlockSpec((1, nc, 8, DK), lambda b, i: (b, nt - 1 - i, 0, 0))]
    return pl.pallas_call(
        functools.partial(_pre_kernel, n_tiles=nt),
        grid=(B, nt),
        in_specs=in_specs,
        out_specs=out_specs,
        out_shape=out_shape,
        scratch_shapes=[pltpu.VMEM((HEADS, HV, HK), F32), pltpu.VMEM((T, D), BF16)],
        compiler_params=pltpu.CompilerParams(
            dimension_semantics=("arbitrary", "arbitrary"), vmem_limit_bytes=VMEM_LIMIT),
        name="encoder_pre",
    )(x, x, x, mod3, ng, wp, wp, wp, wp, wp, wp, wlr, wma, cw, cb, wg, bg, wao)


def _post_kernel(x_ref, qfb_ref, kft_ref, kbt_ref, v_ref, hb_ref, yam_ref,
                 sbt_ref, uft_ref, cvec_ref, mod_ref, gln_ref, wbz_ref, wmb_ref, wbo_ref, wo_ref, fng_ref,
                 o_ref, state_ref, yin_ref):
    i = pl.program_id(1)
    T = x_ref.shape[1]
    n_chunks = T // CHUNK

    @pl.when(i == 0)
    def _():
        state_ref[...] = jnp.zeros_like(state_ref)

    r = lax.broadcasted_iota(jnp.int32, (CHUNK, CHUNK), 0)
    c = lax.broadcasted_iota(jnp.int32, (CHUNK, CHUNK), 1)
    causal = c <= r
    zeros = jnp.zeros((CHUNK, HK), BF16)

    att = {}
    for ci in range(n_chunks):
        rows = slice(ci * CHUNK, (ci + 1) * CHUNK)
        for h in range(HEADS):
            kc = slice(h * HK, (h + 1) * HK)
            kblk = jnp.concatenate(
                [jnp.concatenate([kft_ref[0, ci, kc, :], zeros], axis=1),
                 jnp.concatenate([zeros, kbt_ref[0, ci, kc, :]], axis=1)], axis=0)
            a2 = _dot(qfb_ref[0, rows, 2 * h * HK:(2 * h + 2) * HK], kblk)
            att[ci, h] = jnp.where(causal, a2[:, :CHUNK], a2[:, CHUNK:]).astype(BF16)

    sb = _silu(_dot(hb_ref[0], _wb(wbz_ref[...])))

    scat = {}
    for h in range(HEADS):
        kc = slice(h * HK, (h + 1) * HK)
        vc = slice(h * HV, (h + 1) * HV)
        st = state_ref[h]
        for ci in range(n_chunks):
            cv = cvec_ref[0, ci]
            scat[ci, h] = jnp.concatenate(
                [jnp.transpose(st * cv[1:2, kc]).astype(BF16), sbt_ref[0, ci, kc, :]], axis=0)
            st = st * cv[0:1, kc] + uft_ref[0, ci, vc, :].astype(F32)
        state_ref[h] = st

    for ci in range(n_chunks):
        rows = slice(ci * CHUNK, (ci + 1) * CHUNK)
        for h in range(HEADS):
            vc = slice(h * HV, (h + 1) * HV)
            o = (_dot(att[ci, h], v_ref[0, rows, vc])
                 + _dot(qfb_ref[0, rows, 2 * h * HK:(2 * h + 2) * HK], scat[ci, h]))
            o = o * _rms_scale(o) * gln_ref[:, vc]
            yin_ref[rows, vc] = (o * sb[rows, vc]).astype(BF16)

    smb = _sigmoid(_dot(hb_ref[0], _wb(wmb_ref[...])))
    half = T // 2
    merged = {}
    for hf in range(2):
        rows = slice(hf * half, (hf + 1) * half)
        y_b = _dot(yin_ref[rows, :], _wb(wbo_ref[...]))
        merged[hf] = (yam_ref[0, rows, :].astype(F32) + smb[rows] * y_b).astype(BF16)
    for hf in range(2):
        rows = slice(hf * half, (hf + 1) * half)
        y = x_ref[0, rows, :] + mod_ref[0, 2:3, :] * _dot(merged[hf], _wb(wo_ref[...]))
        o_ref[0, rows, :] = y * _rms_scale(y) * fng_ref[...]


def _post(x, qfb, kft, kbt, v, hb, yam, sbt, uft, cvec, mod3, gln, wbz, wmb, wbo, wo, fng):
    B, L, _ = x.shape
    T = TILE
    nt = L // T
    nc = T // CHUNK
    tok = lambda w: pl.BlockSpec((1, T, w), lambda b, i: (b, i, 0))
    st_spec = pl.BlockSpec((1, nc, HEADS * HV, HK), lambda b, i: (b, i, 0, 0))
    kt_spec = pl.BlockSpec((1, nc, DK, CHUNK), lambda b, i: (b, i, 0, 0))
    sb_spec = pl.BlockSpec((1, nc, DK, HV), lambda b, i: (b, i, 0, 0))
    wcol = lambda j: pl.BlockSpec((D // 2, D), lambda b, i: (0, j), pipeline_mode=pl.Buffered(1))
    in_specs = [tok(D), tok(2 * DK), kt_spec, kt_spec, tok(D), tok(D), tok(D),
                sb_spec, st_spec,
                pl.BlockSpec((1, nc, 8, DK), lambda b, i: (b, i, 0, 0)),
                pl.BlockSpec((1, 3, D), lambda b, i: (b, 0, 0)),
                _const_spec((1, D)), wcol(6), wcol(1),
                _const_spec((D // 2, D)), _const_spec((D // 2, D)), _const_spec((1, D))]
    return pl.pallas_call(
        _post_kernel,
        grid=(B, nt),
        in_specs=in_specs,
        out_specs=tok(D),
        out_shape=jax.ShapeDtypeStruct((B, L, D), F32),
        scratch_shapes=[pltpu.VMEM((HEADS, HV, HK), F32), pltpu.VMEM((T, D), BF16)],
        compiler_params=pltpu.CompilerParams(
            dimension_semantics=("arbitrary", "arbitrary"), vmem_limit_bytes=VMEM_LIMIT),
        name="encoder_post",
    )(x, qfb, kft, kbt, v, hb, yam, sbt, uft, cvec, mod3, gln, wbz, wmb, wbo, wo, fng)


def _pack_kernel(w_ref, o_ref):
    k = w_ref.shape[0]
    wb = w_ref[...].astype(BF16)
    r = lax.broadcasted_iota(jnp.int32, (k // 2, k), 0)
    c = lax.broadcasted_iota(jnp.int32, (k // 2, k), 1)
    even = lax.bitcast_convert_type(_dot((c == 2 * r).astype(BF16), wb), jnp.uint32)
    odd = lax.bitcast_convert_type(_dot((c == 2 * r + 1).astype(BF16), wb), jnp.uint32)
    o_ref[...] = (even >> 16) | odd


def _pack_row_pairs(w32, n_cols):
    k = w32.shape[0]
    return pl.pallas_call(
        _pack_kernel,
        grid=(n_cols // D,),
        in_specs=[pl.BlockSpec((k, D), lambda j: (0, j))],
        out_specs=pl.BlockSpec((k // 2, D), lambda j: (0, j)),
        out_shape=jax.ShapeDtypeStruct((k // 2, n_cols), jnp.uint32),
        compiler_params=pltpu.CompilerParams(vmem_limit_bytes=VMEM_LIMIT),
        name="pack_weights",
    )(w32)


def kernel(x_prompt, x_sample, c_prompt, c_sample, w_ada, b_ada, norm_g, w_in, conv_w, conv_b,
           w_gate_f, b_gate_f, w_gate_b, b_gate_b, gla_norm_g, w_a_out, w_b_out, w_out,
           final_norm_g):
    assert w_ada.shape[0] == 1, "one layer"
    o_q = 4 * D
    o_lr = o_q + 2 * DK + 2 * D
    o_m = o_lr + 2 * RANK
    wp = _pack_row_pairs(w_in[0], o_lr)
    wlr = jnp.pad(w_in[0][:, o_lr:o_m].astype(BF16), ((0, 0), (0, 128 - 2 * RANK)))
    wpm = _pack_row_pairs(w_in[0][:, o_m:], 2 * D)
    wg = jnp.zeros((128, 2 * DK), F32)
    wg = wg.at[:RANK, :DK].set(w_gate_f[0]).at[RANK:2 * RANK, DK:].set(w_gate_b[0]).astype(BF16)
    bg = jnp.concatenate([b_gate_f[0], b_gate_b[0]])[None, :]
    ng = norm_g[0][None, :]
    cw = conv_w[0]
    cb = conv_b[0][None, :]
    gln = gla_norm_g[0][None, :]
    wao = _pack_row_pairs(w_a_out[0], D)
    wbo = _pack_row_pairs(w_b_out[0], D)
    wo = _pack_row_pairs(w_out[0], D)
    fng = final_norm_g[None, :]

    nb = x_prompt.shape[0]
    mod = _modulation(jnp.concatenate([c_prompt, c_sample], axis=0), w_ada[0], b_ada[0][None, :])
    mod3 = mod.reshape(mod.shape[0], 3, D)

    def encode(x, m3):
        pre = _pre(x, m3, ng, wp, wlr, wpm, cw, cb, wg, bg, wao)
        return _post(x, *pre, m3, gln, wp, wpm, wbo, wo, fng)

    return (encode(x_prompt, mod3[:nb]), encode(x_sample, mod3[nb:]))
```

```python
import functools

import jax
import jax.numpy as jnp
from jax import lax
from jax.experimental import pallas as pl
from jax.experimental.pallas import tpu as pltpu

D = 1024
HEADS = 4
DK = 512
HK = DK // HEADS
HV = D // HEADS
RANK = 16
GATE_TEMP = 16.0
EPS = 1e-6
CHUNK = 128
TILE = 512
HALO = 16
SLAB = 256
LOG2E = 1.4426950408889634
EXP2_CLAMP = 115.0
VMEM_LIMIT = 58 * 1024 * 1024

F32 = jnp.float32
BF16 = jnp.bfloat16


def _dot(a, b):
    return jnp.dot(a, b, preferred_element_type=F32)


def _wb(w_u32):
    return pltpu.bitcast(w_u32, BF16)


def _sigmoid(x):
    return 0.5 + 0.5 * jnp.tanh(0.5 * x)


def _silu(x):
    h = 0.5 * x
    return h + h * jnp.tanh(h)


def _rms_scale(x):
    return lax.rsqrt(jnp.mean(x * x, axis=-1, keepdims=True) + EPS)


def _const_spec(shape):
    zeros = (0,) * len(shape)
    return pl.BlockSpec(shape, lambda b, i: zeros, pipeline_mode=pl.Buffered(1))


def _mod_kernel(c_ref, w_ref, b_ref, o_ref):
    o_ref[...] = jnp.dot(_silu(c_ref[...]), w_ref[...], preferred_element_type=F32,
                         precision=lax.Precision.HIGHEST) + b_ref[...]


def _modulation(c, w_ada, b_ada):
    n = c.shape[0]
    return pl.pallas_call(
        _mod_kernel,
        out_shape=jax.ShapeDtypeStruct((n, 3 * D), F32),
        compiler_params=pltpu.CompilerParams(vmem_limit_bytes=VMEM_LIMIT),
        name="adaln_mod",
    )(c, w_ada, b_ada)


def _pre_kernel(x_ref, xp_ref, xn_ref, mod_ref, ng_ref, wab_ref, wac_ref, wax_ref, waz_ref,
                wqk_ref, wv_ref, wlr_ref, wma_ref,
                cw_ref, cb_ref, wg_ref, bg_ref, wao_ref,
                qfb_ref, kft_ref, kbt_ref, v_ref, hb_ref, yam_ref,
                sbs_ref, uft_ref, cvec_ref, state_ref, ga_ref, *, n_tiles):
    i = pl.program_id(1)
    tile = n_tiles - 1 - i
    T = x_ref.shape[1]

    @pl.when(i == 0)
    def _():
        state_ref[...] = jnp.zeros_like(state_ref)

    shift = mod_ref[0, 0:1, :]
    scale1 = 1.0 + mod_ref[0, 1:2, :]
    ng = ng_ref[...]

    def normed(xf):
        return ((xf * _rms_scale(xf) * ng) * scale1 + shift).astype(BF16)

    hb = normed(x_ref[0])
    hb_ref[0] = hb
    zero_halo = jnp.zeros((HALO, D), BF16)
    hprev = jnp.where(tile > 0, normed(xp_ref[0]), zero_halo)
    hnext = jnp.where(tile < n_tiles - 1, normed(xn_ref[0]), zero_halo)
    hx = jnp.concatenate([hprev, hb, hnext], axis=0)

    lr = _dot(hb, wlr_ref[...]).astype(BF16)
    qk = _dot(hb, _wb(wqk_ref[...]))
    z = _dot(lr, wg_ref[...]) + bg_ref[...]
    g2 = (jnp.minimum(z, 0.0) - jnp.log1p(jnp.exp(-jnp.abs(z)))) * (LOG2E / GATE_TEMP)

    q = qk[:, :DK] * (HK ** -0.5)
    k = qk[:, DK:2 * DK]
    vb = _dot(hb, _wb(wv_ref[...])).astype(BF16)
    v_ref[0] = vb
    vt = jnp.transpose(vb)

    r = lax.broadcasted_iota(jnp.int32, (CHUNK, 2 * CHUNK), 0)
    c = lax.broadcasted_iota(jnp.int32, (CHUNK, 2 * CHUNK), 1) % CHUNK
    tri_incl = (c <= r).astype(BF16)
    tri_strict = (c < r).astype(BF16)
    mid = CHUNK // 2
    hilo_of_chunk = {}
    staged = {}

    def pair_cols(hp):
        return slice(hp * 2 * HK, (hp + 1) * 2 * HK), slice(DK + hp * 2 * HK, DK + (hp + 1) * 2 * HK)

    def gla_cumsum(ci, hp):
        rows = slice(ci * CHUNK, (ci + 1) * CHUNK)
        if ci not in hilo_of_chunk:
            gc = g2[rows]
            hi = gc.astype(BF16)
            lo = (gc - hi.astype(F32)).astype(BF16)
            hilo_of_chunk[ci] = (jnp.concatenate([hi, lo], axis=0), gc)
        hilo, _ = hilo_of_chunk[ci]
        pc, pcb = pair_cols(hp)
        staged[ci, hp, 'cum'] = (_dot(tri_incl, hilo[:, pc]), _dot(tri_strict, hilo[:, pcb]))

    def gla_operands(ci, hp):
        rows = slice(ci * CHUNK, (ci + 1) * CHUNK)
        pc, pcb = pair_cols(hp)
        gc = hilo_of_chunk[ci][1]
        gf, eb = staged.pop((ci, hp, 'cum'))
        qc, kc = q[rows, pc], k[rows, pc]
        gm = gf[mid - 1:mid, :]
        gl = gf[CHUNK - 1:CHUNK, :]
        ef = jnp.exp2(lax.clamp(-EXP2_CLAMP, gf - gm, EXP2_CLAMP))
        qaf = (qc * ef).astype(BF16)
        kaf = kc * (1.0 / ef)
        kft_ref[0, ci, pc, :] = jnp.transpose(kaf.astype(BF16))
        em = eb[mid:mid + 1, :]
        et = eb[CHUNK - 1:CHUNK, :] + gc[CHUNK - 1:CHUNK, pcb]
        ebx = jnp.exp2(lax.clamp(-EXP2_CLAMP, em - eb, EXP2_CLAMP))
        qab = (qc * ebx).astype(BF16)
        kab = kc * (1.0 / ebx)
        kbt_ref[0, ci, pc, :] = jnp.transpose(kab.astype(BF16))
        cvec_ref[0, ci, :, pc] = jnp.concatenate(
            [jnp.exp2(gl), jnp.exp2(gm), jnp.zeros((6, 2 * HK), F32)], axis=0)
        for hh in range(2):
            h = 2 * hp + hh
            lc = slice(hh * HK, (hh + 1) * HK)
            qfb_ref[0, rows, 2 * h * HK:(2 * h + 1) * HK] = qaf[:, lc]
            qfb_ref[0, rows, (2 * h + 1) * HK:(2 * h + 2) * HK] = qab[:, lc]
        staged[ci, hp, 'ks'] = ((kaf * jnp.exp2(gl - gm)).astype(BF16), (kab * jnp.exp2(em)).astype(BF16),
                                jnp.exp2(et - em), jnp.exp2(et))

    def gla_state(ci, hp):
        rows = slice(ci * CHUNK, (ci + 1) * CHUNK)
        ksf, ksb, mu_b, gamma_b = staged.pop((ci, hp, 'ks'))
        for hh in range(2):
            h = 2 * hp + hh
            lc = slice(hh * HK, (hh + 1) * HK)
            vc = slice(h * HV, (h + 1) * HV)
            u = _dot(vt[vc, rows], jnp.concatenate([ksf[:, lc], ksb[:, lc]], axis=1))
            uft_ref[0, ci, vc, :] = u[:, :HK].astype(BF16)
            st = state_ref[h]
            sbs_ref[0, ci, h * HK:(h + 1) * HK, :] = jnp.transpose(st * mu_b[:, lc]).astype(BF16)
            state_ref[h] = st * gamma_b[:, lc] + u[:, HK:]

    inner = slice(HALO, T + HALO)
    slab_vals = {}

    def conv_piece(s, part):
        cols = slice(s * SLAB, (s + 1) * SLAB)
        if part == 0:
            slab_vals[s, 'c'] = _dot(hx, _wb(wac_ref[:, cols]))
        elif part == 1:
            p = slab_vals.pop((s, 'c')) * _dot(hx, _wb(wax_ref[:, cols]))
            p_prev = pltpu.roll(p, 1, axis=0)[inner]
            p_next = pltpu.roll(p, T + 2 * HALO - 1, axis=0)[inner]
            slab_vals[s, 'u'] = (cw_ref[0:1, cols] * p_prev + cw_ref[1:2, cols] * p[inner]
                                 + cw_ref[2:3, cols] * p_next + cb_ref[:, cols])
        elif part == 2:
            slab_vals[s, 'bu'] = _dot(hb, _wb(wab_ref[:, cols])) * slab_vals.pop((s, 'u'))
        else:
            az = _dot(hb, _wb(waz_ref[:, cols]))
            ga_ref[:, cols] = (slab_vals.pop((s, 'bu')) * _silu(az)).astype(BF16)

    gla_pieces = [(ci, hp) for ci in reversed(range(T // CHUNK)) for hp in range(HEADS // 2)]
    conv_pieces = [(s, part) for s in range(D // SLAB) for part in range(4)]
    assert len(conv_pieces) == 2 * len(gla_pieces)
    gla_cumsum(*gla_pieces[0])
    for j, gp in enumerate(gla_pieces):
        conv_piece(*conv_pieces[2 * j])
        gla_operands(*gp)
        if j + 1 < len(gla_pieces):
            gla_cumsum(*gla_pieces[j + 1])
        conv_piece(*conv_pieces[2 * j + 1])
        gla_state(*gp)

    for hc in range(2):
        cols = slice(hc * D // 2, (hc + 1) * D // 2)
        m_a = _dot(hb, _wb(wma_ref[:, cols]))
        y_a = _dot(ga_ref[...], _wb(wao_ref[:, cols]))
        yam_ref[0, :, cols] = (_sigmoid(m_a) * y_a).astype(BF16)


def _pre(x, mod3, ng, wp, wlr, wma, cw, cb, wg, bg, wao):
    B, L, _ = x.shape
    T = TILE
    nt = L // T
    nc = T // CHUNK
    tok = lambda w: pl.BlockSpec((1, T, w), lambda b, i: (b, nt - 1 - i, 0))
    hpt = T // HALO
    wcol = lambda j: pl.BlockSpec((D // 2, D), lambda b, i: (0, j), pipeline_mode=pl.Buffered(1))
    in_specs = [
        tok(D),
        pl.BlockSpec((1, HALO, D), lambda b, i: (b, jnp.maximum((nt - 1 - i) * hpt - 1, 0), 0)),
        pl.BlockSpec((1, HALO, D), lambda b, i: (b, jnp.minimum((nt - i) * hpt, L // HALO - 1), 0)),
        pl.BlockSpec((1, 3, D), lambda b, i: (b, 0, 0)),
        _const_spec((1, D)),
        wcol(0), wcol(1), wcol(2), wcol(3),
        wcol(4), wcol(5),
        _const_spec((D, 128)),
        _const_spec((D // 2, D)),
        _const_spec((3, D)),
        _const_spec((1, D)),
        _const_spec((128, 2 * DK)),
        _const_spec((1, 2 * DK)),
        _const_spec((D // 2, D)),
    ]
    out_shape = [
        jax.ShapeDtypeStruct((B, L, 2 * DK), BF16),
        jax.ShapeDtypeStruct((B, L // CHUNK, DK, CHUNK), BF16),
        jax.ShapeDtypeStruct((B, L // CHUNK, DK, CHUNK), BF16),
        jax.ShapeDtypeStruct((B, L, D), BF16),
        jax.ShapeDtypeStruct((B, L, D), BF16),
        jax.ShapeDtypeStruct((B, L, D), BF16),
        jax.ShapeDtypeStruct((B, L // CHUNK, DK, HV), BF16),
        jax.ShapeDtypeStruct((B, L // CHUNK, HEADS * HV, HK), BF16),
        jax.ShapeDtypeStruct((B, L // CHUNK, 8, DK), F32),
    ]
    st_spec = pl.BlockSpec((1, nc, HEADS * HV, HK), lambda b, i: (b, nt - 1 - i, 0, 0))
    kt_spec = pl.BlockSpec((1, nc, DK, CHUNK), lambda b, i: (b, nt - 1 - i, 0, 0))
    sb_spec = pl.BlockSpec((1, nc, DK, HV), lambda b, i: (b, nt - 1 - i, 0, 0))
    out_specs = [tok(2 * DK), kt_spec, kt_spec, tok(D), tok(D), tok(D), sb_spec, st_spec,
                 pl.BlockSpec((1, nc, 8, DK), lambda b, i: (b, nt - 1 - i, 0, 0))]
    return pl.pallas_call(
        functools.partial(_pre_kernel, n_tiles=nt),
        grid=(B, nt),
        in_specs=in_specs,
        out_specs=out_specs,
        out_shape=out_shape,
        scratch_shapes=[pltpu.VMEM((HEADS, HV, HK), F32), pltpu.VMEM((T, D), BF16)],
        compiler_params=pltpu.CompilerParams(
            dimension_semantics=("arbitrary", "arbitrary"), vmem_limit_bytes=VMEM_LIMIT),
        name="encoder_pre",
    )(x, x, x, mod3, ng, wp, wp, wp, wp, wp, wp, wlr, wma, cw, cb, wg, bg, wao)


def _post_kernel(x_ref, qfb_ref, kft_ref, kbt_ref, v_ref, hb_ref, yam_ref,
                 sbt_ref, uft_ref, cvec_ref, mod_ref, gln_ref, wbz_ref, wmb_ref, wbo_ref, wo_ref, fng_ref,
                 o_ref, state_ref, yin_ref):
    i = pl.program_id(1)
    T = x_ref.shape[1]
    n_chunks = T // CHUNK

    @pl.when(i == 0)
    def _():
        state_ref[...] = jnp.zeros_like(state_ref)

    r = lax.broadcasted_iota(jnp.int32, (CHUNK, CHUNK), 0)
    c = lax.broadcasted_iota(jnp.int32, (CHUNK, CHUNK), 1)
    causal = c <= r
    zeros = jnp.zeros((CHUNK, HK), BF16)

    att = {}
    for ci in range(n_chunks):
        rows = slice(ci * CHUNK, (ci + 1) * CHUNK)
        for h in range(HEADS):
            kc = slice(h * HK, (h + 1) * HK)
            kblk = jnp.concatenate(
                [jnp.concatenate([kft_ref[0, ci, kc, :], zeros], axis=1),
                 jnp.concatenate([zeros, kbt_ref[0, ci, kc, :]], axis=1)], axis=0)
            a2 = _dot(qfb_ref[0, rows, 2 * h * HK:(2 * h + 2) * HK], kblk)
            att[ci, h] = jnp.where(causal, a2[:, :CHUNK], a2[:, CHUNK:]).astype(BF16)

    sb = _silu(_dot(hb_ref[0], _wb(wbz_ref[...])))

    scat = {}
    for h in range(HEADS):
        kc = slice(h * HK, (h + 1) * HK)
        vc = slice(h * HV, (h + 1) * HV)
        st = state_ref[h]
        for ci in range(n_chunks):
            cv = cvec_ref[0, ci]
            scat[ci, h] = jnp.concatenate(
                [jnp.transpose(st * cv[1:2, kc]).astype(BF16), sbt_ref[0, ci, kc, :]], axis=0)
            st = st * cv[0:1, kc] + uft_ref[0, ci, vc, :].astype(F32)
        state_ref[h] = st

    for ci in range(n_chunks):
        rows = slice(ci * CHUNK, (ci + 1) * CHUNK)
        for h in range(HEADS):
            vc = slice(h * HV, (h + 1) * HV)
            o = (_dot(att[ci, h], v_ref[0, rows, vc])
                 + _dot(qfb_ref[0, rows, 2 * h * HK:(2 * h + 2) * HK], scat[ci, h]))
            o = o * _rms_scale(o) * gln_ref[:, vc]
            yin_ref[rows, vc] = (o * sb[rows, vc]).astype(BF16)

    smb = _sigmoid(_dot(hb_ref[0], _wb(wmb_ref[...])))
    half = T // 2
    merged = {}
    for hf in range(2):
        rows = slice(hf * half, (hf + 1) * half)
        y_b = _dot(yin_ref[rows, :], _wb(wbo_ref[...]))
        merged[hf] = (yam_ref[0, rows, :].astype(F32) + smb[rows] * y_b).astype(BF16)
    for hf in range(2):
        rows = slice(hf * half, (hf + 1) * half)
        y = x_ref[0, rows, :] + mod_ref[0, 2:3, :] * _dot(merged[hf], _wb(wo_ref[...]))
        o_ref[0, rows, :] = y * _rms_scale(y) * fng_ref[...]


def _post(x, qfb, kft, kbt, v, hb, yam, sbt, uft, cvec, mod3, gln, wbz, wmb, wbo, wo, fng):
    B, L, _ = x.shape
    T = TILE
    nt = L // T
    nc = T // CHUNK
    tok = lambda w: pl.BlockSpec((1, T, w), lambda b, i: (b, i, 0))
    st_spec = pl.BlockSpec((1, nc, HEADS * HV, HK), lambda b, i: (b, i, 0, 0))
    kt_spec = pl.BlockSpec((1, nc, DK, CHUNK), lambda b, i: (b, i, 0, 0))
    sb_spec = pl.BlockSpec((1, nc, DK, HV), lambda b, i: (b, i, 0, 0))
    wcol = lambda j: pl.BlockSpec((D // 2, D), lambda b, i: (0, j), pipeline_mode=pl.Buffered(1))
    in_specs = [tok(D), tok(2 * DK), kt_spec, kt_spec, tok(D), tok(D), tok(D),
                sb_spec, st_spec,
                pl.BlockSpec((1, nc, 8, DK), lambda b, i: (b, i, 0, 0)),
                pl.BlockSpec((1, 3, D), lambda b, i: (b, 0, 0)),
                _const_spec((1, D)), wcol(6), wcol(1),
                _const_spec((D // 2, D)), _const_spec((D // 2, D)), _const_spec((1, D))]
    return pl.pallas_call(
        _post_kernel,
        grid=(B, nt),
        in_specs=in_specs,
        out_specs=tok(D),
        out_shape=jax.ShapeDtypeStruct((B, L, D), F32),
        scratch_shapes=[pltpu.VMEM((HEADS, HV, HK), F32), pltpu.VMEM((T, D), BF16)],
        compiler_params=pltpu.CompilerParams(
            dimension_semantics=("arbitrary", "arbitrary"), vmem_limit_bytes=VMEM_LIMIT),
        name="encoder_post",
    )(x, qfb, kft, kbt, v, hb, yam, sbt, uft, cvec, mod3, gln, wbz, wmb, wbo, wo, fng)


def _pack_kernel(w_ref, o_ref):
    k = w_ref.shape[0]
    wb = w_ref[...].astype(BF16)
    r = lax.broadcasted_iota(jnp.int32, (k // 2, k), 0)
    c = lax.broadcasted_iota(jnp.int32, (k // 2, k), 1)
    even = lax.bitcast_convert_type(_dot((c == 2 * r).astype(BF16), wb), jnp.uint32)
    odd = lax.bitcast_convert_type(_dot((c == 2 * r + 1).astype(BF16), wb), jnp.uint32)
    o_ref[...] = (even >> 16) | odd


def _pack_row_pairs(w32, n_cols):
    k = w32.shape[1]
    return pl.pallas_call(
        _pack_kernel,
        grid=(n_cols // D,),
        in_specs=[pl.BlockSpec((None, k, D), lambda j: (0, 0, j))],
        out_specs=pl.BlockSpec((k // 2, D), lambda j: (0, j)),
        out_shape=jax.ShapeDtypeStruct((k // 2, n_cols), jnp.uint32),
        compiler_params=pltpu.CompilerParams(vmem_limit_bytes=VMEM_LIMIT),
        name="pack_weights",
    )(w32)


def kernel(x_prompt, x_sample, c_prompt, c_sample, w_ada, b_ada, norm_g, w_in, conv_w, conv_b,
           w_gate_f, b_gate_f, w_gate_b, b_gate_b, gla_norm_g, w_a_out, w_b_out, w_out,
           final_norm_g):
    assert w_ada.shape[0] == 1, "one layer"
    o_q = 4 * D
    o_lr = o_q + 2 * DK + 2 * D
    o_m = o_lr + 2 * RANK
    wp = _pack_row_pairs(w_in, o_lr)
    wlr = jnp.pad(w_in[0, :, o_lr:o_m].astype(BF16), ((0, 0), (0, 128 - 2 * RANK)))
    wpm = _pack_row_pairs(w_in[:, :, o_m:], 2 * D)
    wg = jnp.zeros((128, 2 * DK), F32)
    wg = wg.at[:RANK, :DK].set(w_gate_f[0]).at[RANK:2 * RANK, DK:].set(w_gate_b[0]).astype(BF16)
    bg = jnp.concatenate([b_gate_f[0], b_gate_b[0]])[None, :]
    ng = norm_g[0][None, :]
    cw = conv_w[0]
    cb = conv_b[0][None, :]
    gln = gla_norm_g[0][None, :]
    wao = _pack_row_pairs(w_a_out, D)
    wbo = _pack_row_pairs(w_b_out, D)
    wo = _pack_row_pairs(w_out, D)
    fng = final_norm_g[None, :]

    nb = x_prompt.shape[0]
    mod = _modulation(jnp.concatenate([c_prompt, c_sample], axis=0), w_ada[0], b_ada[0][None, :])
    mod3 = mod.reshape(mod.shape[0], 3, D)

    def encode(x, m3):
        pre = _pre(x, m3, ng, wp, wlr, wpm, cw, cb, wg, bg, wao)
        return _post(x, *pre, m3, gln, wp, wpm, wbo, wo, fng)

    return (encode(x_prompt, mod3[:nb]), encode(x_sample, mod3[nb:]))
```

```python
import functools

import jax
import jax.numpy as jnp
from jax import lax
from jax.experimental import pallas as pl
from jax.experimental.pallas import tpu as pltpu

D = 1024
HEADS = 4
DK = 512
HK = DK // HEADS
HV = D // HEADS
RANK = 16
GATE_TEMP = 16.0
EPS = 1e-6
CHUNK = 128
TILE = 512
HALO = 16
SLAB = 256
LOG2E = 1.4426950408889634
EXP2_CLAMP = 115.0
VMEM_LIMIT = 58 * 1024 * 1024

F32 = jnp.float32
BF16 = jnp.bfloat16


def _dot(a, b):
    return jnp.dot(a, b, preferred_element_type=F32)


def _sigmoid(x):
    return 0.5 + 0.5 * jnp.tanh(0.5 * x)


def _silu(x):
    h = 0.5 * x
    return h + h * jnp.tanh(h)


def _rms_scale(x):
    return lax.rsqrt(jnp.mean(x * x, axis=-1, keepdims=True) + EPS)


def _const_spec(shape):
    zeros = (0,) * len(shape)
    return pl.BlockSpec(shape, lambda b, i: zeros, pipeline_mode=pl.Buffered(1))


def _mod_kernel(c_ref, w_ref, b_ref, o_ref):
    o_ref[...] = jnp.dot(_silu(c_ref[...]), w_ref[...], preferred_element_type=F32,
                         precision=lax.Precision.HIGHEST) + b_ref[...]


def _modulation(c, w_ada, b_ada):
    n = c.shape[0]
    return pl.pallas_call(
        _mod_kernel,
        out_shape=jax.ShapeDtypeStruct((n, 3 * D), F32),
        compiler_params=pltpu.CompilerParams(vmem_limit_bytes=VMEM_LIMIT),
        name="adaln_mod",
    )(c, w_ada, b_ada)


def _pre_kernel(x_ref, xp_ref, xn_ref, mod_ref, ng_ref, wab_ref, wac_ref, wax_ref, waz_ref,
                wqk_ref, wv_ref, wlr_ref, wma_ref,
                cw_ref, cb_ref, wg_ref, bg_ref, wao_ref,
                qfb_ref, kft_ref, kbt_ref, v_ref, hb_ref, yam_ref,
                sbs_ref, uft_ref, cvec_ref, state_ref, ga_ref, *, n_tiles):
    i = pl.program_id(1)
    tile = n_tiles - 1 - i
    T = x_ref.shape[1]

    @pl.when(i == 0)
    def _():
        state_ref[...] = jnp.zeros_like(state_ref)

    shift = mod_ref[0, 0:1, :]
    gain = ng_ref[...] * (1.0 + mod_ref[0, 1:2, :])

    def normed(xf):
        return (xf * _rms_scale(xf) * gain + shift).astype(BF16)

    hb = normed(x_ref[0])
    hb_ref[0] = hb
    zero_halo = jnp.zeros((HALO, D), BF16)
    hprev = jnp.where(tile > 0, normed(xp_ref[0]), zero_halo)
    hnext = jnp.where(tile < n_tiles - 1, normed(xn_ref[0]), zero_halo)
    hx = jnp.concatenate([hprev, hb, hnext], axis=0)

    lr = _dot(hb, wlr_ref[...]).astype(BF16)
    qk = _dot(hb, wqk_ref[...])
    z = _dot(lr, wg_ref[...]) + bg_ref[...]
    g2 = (jnp.minimum(z, 0.0) - jnp.log1p(jnp.exp(-jnp.abs(z)))) * (LOG2E / GATE_TEMP)

    q = qk[:, :DK] * (HK ** -0.5)
    k = qk[:, DK:2 * DK]
    vb = _dot(hb, wv_ref[...]).astype(BF16)
    v_ref[0] = vb
    vt = jnp.transpose(vb)

    r = lax.broadcasted_iota(jnp.int32, (CHUNK, 2 * CHUNK), 0)
    c = lax.broadcasted_iota(jnp.int32, (CHUNK, 2 * CHUNK), 1) % CHUNK
    tri_incl = (c <= r).astype(BF16)
    tri_strict = (c < r).astype(BF16)
    mid = CHUNK // 2
    hilo_of_chunk = {}
    staged = {}

    def pair_cols(hp):
        return slice(hp * 2 * HK, (hp + 1) * 2 * HK), slice(DK + hp * 2 * HK, DK + (hp + 1) * 2 * HK)

    def gla_cumsum(ci, hp):
        rows = slice(ci * CHUNK, (ci + 1) * CHUNK)
        if ci not in hilo_of_chunk:
            gc = g2[rows]
            hi = gc.astype(BF16)
            lo = (gc - hi.astype(F32)).astype(BF16)
            hilo_of_chunk[ci] = (jnp.concatenate([hi, lo], axis=0), gc)
        hilo, _ = hilo_of_chunk[ci]
        pc, pcb = pair_cols(hp)
        staged[ci, hp, 'cum'] = (_dot(tri_incl, hilo[:, pc]), _dot(tri_strict, hilo[:, pcb]))

    def gla_operands(ci, hp):
        rows = slice(ci * CHUNK, (ci + 1) * CHUNK)
        pc, pcb = pair_cols(hp)
        gc = hilo_of_chunk[ci][1]
        gf, eb = staged.pop((ci, hp, 'cum'))
        qc, kc = q[rows, pc], k[rows, pc]
        gm = gf[mid - 1:mid, :]
        gl = gf[CHUNK - 1:CHUNK, :]
        ef = jnp.exp2(lax.clamp(-EXP2_CLAMP, gf - gm, EXP2_CLAMP))
        qaf = (qc * ef).astype(BF16)
        kaf = kc * (1.0 / ef)
        kft_ref[0, ci, pc, :] = jnp.transpose(kaf.astype(BF16))
        em = eb[mid:mid + 1, :]
        et = eb[CHUNK - 1:CHUNK, :] + gc[CHUNK - 1:CHUNK, pcb]
        ebx = jnp.exp2(lax.clamp(-EXP2_CLAMP, em - eb, EXP2_CLAMP))
        qab = (qc * ebx).astype(BF16)
        kab = kc * (1.0 / ebx)
        kbt_ref[0, ci, pc, :] = jnp.transpose(kab.astype(BF16))
        cvec_ref[0, ci, :, pc] = jnp.concatenate(
            [jnp.exp2(gl), jnp.exp2(gm), jnp.zeros((6, 2 * HK), F32)], axis=0)
        for hh in range(2):
            h = 2 * hp + hh
            lc = slice(hh * HK, (hh + 1) * HK)
            qfb_ref[0, rows, 2 * h * HK:(2 * h + 1) * HK] = qaf[:, lc]
            qfb_ref[0, rows, (2 * h + 1) * HK:(2 * h + 2) * HK] = qab[:, lc]
        staged[ci, hp, 'ks'] = ((kaf * jnp.exp2(gl - gm)).astype(BF16), (kab * jnp.exp2(em)).astype(BF16),
                                jnp.exp2(et - em), jnp.exp2(et))

    def gla_state(ci, hp):
        rows = slice(ci * CHUNK, (ci + 1) * CHUNK)
        ksf, ksb, mu_b, gamma_b = staged.pop((ci, hp, 'ks'))
        for hh in range(2):
            h = 2 * hp + hh
            lc = slice(hh * HK, (hh + 1) * HK)
            vc = slice(h * HV, (h + 1) * HV)
            u = _dot(vt[vc, rows], jnp.concatenate([ksf[:, lc], ksb[:, lc]], axis=1))
            uft_ref[0, ci, vc, :] = u[:, :HK].astype(BF16)
            st = state_ref[h]
            sbs_ref[0, ci, h * HK:(h + 1) * HK, :] = jnp.transpose(st * mu_b[:, lc]).astype(BF16)
            state_ref[h] = st * gamma_b[:, lc] + u[:, HK:]

    inner = slice(HALO, T + HALO)
    slab_vals = {}

    def conv_piece(s, part):
        cols = slice(s * SLAB, (s + 1) * SLAB)
        if part == 0:
            slab_vals[s, 'c'] = _dot(hx, wac_ref[:, cols])
        elif part == 1:
            p = slab_vals.pop((s, 'c')) * _dot(hx, wax_ref[:, cols])
            p_prev = pltpu.roll(p, 1, axis=0)[inner]
            p_next = pltpu.roll(p, T + 2 * HALO - 1, axis=0)[inner]
            slab_vals[s, 'u'] = (cw_ref[0:1, cols] * p_prev + cw_ref[1:2, cols] * p[inner]
                                 + cw_ref[2:3, cols] * p_next + cb_ref[:, cols])
        elif part == 2:
            slab_vals[s, 'bu'] = _dot(hb, wab_ref[:, cols]) * slab_vals.pop((s, 'u'))
        else:
            az = _dot(hb, waz_ref[:, cols])
            ga_ref[:, cols] = (slab_vals.pop((s, 'bu')) * _silu(az)).astype(BF16)

    gla_pieces = [(ci, hp) for ci in reversed(range(T // CHUNK)) for hp in range(HEADS // 2)]
    conv_pieces = [(s, part) for s in range(D // SLAB) for part in range(4)]
    assert len(conv_pieces) == 2 * len(gla_pieces)
    gla_cumsum(*gla_pieces[0])
    for j, gp in enumerate(gla_pieces):
        conv_piece(*conv_pieces[2 * j])
        gla_operands(*gp)
        if j + 1 < len(gla_pieces):
            gla_cumsum(*gla_pieces[j + 1])
        conv_piece(*conv_pieces[2 * j + 1])
        gla_state(*gp)

    for hc in range(2):
        cols = slice(hc * D // 2, (hc + 1) * D // 2)
        m_a = _dot(hb, wma_ref[:, cols])
        y_a = _dot(ga_ref[...], wao_ref[:, cols])
        yam_ref[0, :, cols] = (_sigmoid(m_a) * y_a).astype(BF16)


def _pre(x, mod3, ng, w, wlr, wma, cw, cb, wg, bg, wao):
    B, L, _ = x.shape
    T = TILE
    nt = L // T
    nc = T // CHUNK
    tok = lambda w: pl.BlockSpec((1, T, w), lambda b, i: (b, nt - 1 - i, 0))
    hpt = T // HALO
    wcol = lambda j: pl.BlockSpec((D, D), lambda b, i: (0, j), pipeline_mode=pl.Buffered(1))
    in_specs = [
        tok(D),
        pl.BlockSpec((1, HALO, D), lambda b, i: (b, jnp.maximum((nt - 1 - i) * hpt - 1, 0), 0)),
        pl.BlockSpec((1, HALO, D), lambda b, i: (b, jnp.minimum((nt - i) * hpt, L // HALO - 1), 0)),
        pl.BlockSpec((1, 3, D), lambda b, i: (b, 0, 0)),
        _const_spec((1, D)),
        wcol(0), wcol(1), wcol(2), wcol(3),
        wcol(4), wcol(5),
        _const_spec((D, 128)),
        _const_spec((D, D)),
        _const_spec((3, D)),
        _const_spec((1, D)),
        _const_spec((128, 2 * DK)),
        _const_spec((1, 2 * DK)),
        _const_spec((D, D)),
    ]
    out_shape = [
        jax.ShapeDtypeStruct((B, L, 2 * DK), BF16),
        jax.ShapeDtypeStruct((B, L // CHUNK, DK, CHUNK), BF16),
        jax.ShapeDtypeStruct((B, L // CHUNK, DK, CHUNK), BF16),
        jax.ShapeDtypeStruct((B, L, D), BF16),
        jax.ShapeDtypeStruct((B, L, D), BF16),
        jax.ShapeDtypeStruct((B, L, D), BF16),
        jax.ShapeDtypeStruct((B, L // CHUNK, DK, HV), BF16),
        jax.ShapeDtypeStruct((B, L // CHUNK, HEADS * HV, HK), BF16),
        jax.ShapeDtypeStruct((B, L // CHUNK, 8, DK), F32),
    ]
    st_spec = pl.BlockSpec((1, nc, HEADS * HV, HK), lambda b, i: (b, nt - 1 - i, 0, 0))
    kt_spec = pl.BlockSpec((1, nc, DK, CHUNK), lambda b, i: (b, nt - 1 - i, 0, 0))
    sb_spec = pl.BlockSpec((1, nc, DK, HV), lambda b, i: (b, nt - 1 - i, 0, 0))
    out_specs = [tok(2 * DK), kt_spec, kt_spec, tok(D), tok(D), tok(D), sb_spec, st_spec,
                 pl.BlockSpec((1, nc, 8, DK), lambda b, i: (b, nt - 1 - i, 0, 0))]
    return pl.pallas_call(
        functools.partial(_pre_kernel, n_tiles=nt),
        grid=(B, nt),
        in_specs=in_specs,
        out_specs=out_specs,
        out_shape=out_shape,
        scratch_shapes=[pltpu.VMEM((HEADS, HV, HK), F32), pltpu.VMEM((T, D), BF16)],
        compiler_params=pltpu.CompilerParams(
            dimension_semantics=("arbitrary", "arbitrary"), vmem_limit_bytes=VMEM_LIMIT),
        name="encoder_pre",
    )(x, x, x, mod3, ng, w, w, w, w, w, w, wlr, wma, cw, cb, wg, bg, wao)


def _post_kernel(x_ref, qfb_ref, kft_ref, kbt_ref, v_ref, hb_ref, yam_ref,
                 sbt_ref, uft_ref, cvec_ref, mod_ref, gln_ref, wbz_ref, wmb_ref, wbo_ref, wo_ref, fng_ref,
                 o_ref, state_ref, yin_ref):
    i = pl.program_id(1)
    T = x_ref.shape[1]
    n_chunks = T // CHUNK

    @pl.when(i == 0)
    def _():
        state_ref[...] = jnp.zeros_like(state_ref)

    r = lax.broadcasted_iota(jnp.int32, (CHUNK, CHUNK), 0)
    c = lax.broadcasted_iota(jnp.int32, (CHUNK, CHUNK), 1)
    causal = c <= r
    zeros = jnp.zeros((CHUNK, HK), BF16)

    att = {}
    for ci in range(n_chunks):
        rows = slice(ci * CHUNK, (ci + 1) * CHUNK)
        for h in range(HEADS):
            kc = slice(h * HK, (h + 1) * HK)
            kblk = jnp.concatenate(
                [jnp.concatenate([kft_ref[0, ci, kc, :], zeros], axis=1),
                 jnp.concatenate([zeros, kbt_ref[0, ci, kc, :]], axis=1)], axis=0)
            a2 = _dot(qfb_ref[0, rows, 2 * h * HK:(2 * h + 2) * HK], kblk)
            att[ci, h] = jnp.where(causal, a2[:, :CHUNK], a2[:, CHUNK:]).astype(BF16)

    sb = _silu(_dot(hb_ref[0], wbz_ref[...]))

    scat = {}
    for h in range(HEADS):
        kc = slice(h * HK, (h + 1) * HK)
        vc = slice(h * HV, (h + 1) * HV)
        st = state_ref[h]
        for ci in range(n_chunks):
            cv = cvec_ref[0, ci]
            scat[ci, h] = jnp.concatenate(
                [jnp.transpose(st * cv[1:2, kc]).astype(BF16), sbt_ref[0, ci, kc, :]], axis=0)
            st = st * cv[0:1, kc] + uft_ref[0, ci, vc, :].astype(F32)
        state_ref[h] = st

    for ci in range(n_chunks):
        rows = slice(ci * CHUNK, (ci + 1) * CHUNK)
        for h in range(HEADS):
            vc = slice(h * HV, (h + 1) * HV)
            o = (_dot(att[ci, h], v_ref[0, rows, vc])
                 + _dot(qfb_ref[0, rows, 2 * h * HK:(2 * h + 2) * HK], scat[ci, h]))
            o = o * _rms_scale(o) * gln_ref[:, vc]
            yin_ref[rows, vc] = (o * sb[rows, vc]).astype(BF16)

    smb = _sigmoid(_dot(hb_ref[0], wmb_ref[...]))
    half = T // 2
    merged = {}
    for hf in range(2):
        rows = slice(hf * half, (hf + 1) * half)
        y_b = _dot(yin_ref[rows, :], wbo_ref[...])
        merged[hf] = (yam_ref[0, rows, :].astype(F32) + smb[rows] * y_b).astype(BF16)
    for hf in range(2):
        rows = slice(hf * half, (hf + 1) * half)
        y = x_ref[0, rows, :] + mod_ref[0, 2:3, :] * _dot(merged[hf], wo_ref[...])
        o_ref[0, rows, :] = y * _rms_scale(y) * fng_ref[...]


def _post(x, qfb, kft, kbt, v, hb, yam, sbt, uft, cvec, mod3, gln, wbz, wmb, wbo, wo, fng):
    B, L, _ = x.shape
    T = TILE
    nt = L // T
    nc = T // CHUNK
    tok = lambda w: pl.BlockSpec((1, T, w), lambda b, i: (b, i, 0))
    st_spec = pl.BlockSpec((1, nc, HEADS * HV, HK), lambda b, i: (b, i, 0, 0))
    kt_spec = pl.BlockSpec((1, nc, DK, CHUNK), lambda b, i: (b, i, 0, 0))
    sb_spec = pl.BlockSpec((1, nc, DK, HV), lambda b, i: (b, i, 0, 0))
    in_specs = [tok(D), tok(2 * DK), kt_spec, kt_spec, tok(D), tok(D), tok(D),
                sb_spec, st_spec,
                pl.BlockSpec((1, nc, 8, DK), lambda b, i: (b, i, 0, 0)),
                pl.BlockSpec((1, 3, D), lambda b, i: (b, 0, 0)),
                _const_spec((1, D)), _const_spec((D, D)), _const_spec((D, D)), _const_spec((D, D)),
                _const_spec((D, D)), _const_spec((1, D))]
    return pl.pallas_call(
        _post_kernel,
        grid=(B, nt),
        in_specs=in_specs,
        out_specs=tok(D),
        out_shape=jax.ShapeDtypeStruct((B, L, D), F32),
        scratch_shapes=[pltpu.VMEM((HEADS, HV, HK), F32), pltpu.VMEM((T, D), BF16)],
        compiler_params=pltpu.CompilerParams(
            dimension_semantics=("arbitrary", "arbitrary"), vmem_limit_bytes=VMEM_LIMIT),
        name="encoder_post",
    )(x, qfb, kft, kbt, v, hb, yam, sbt, uft, cvec, mod3, gln, wbz, wmb, wbo, wo, fng)


def kernel(x_prompt, x_sample, c_prompt, c_sample, w_ada, b_ada, norm_g, w_in, conv_w, conv_b,
           w_gate_f, b_gate_f, w_gate_b, b_gate_b, gla_norm_g, w_a_out, w_b_out, w_out,
           final_norm_g):
    assert w_ada.shape[0] == 1, "one layer"
    w = w_in[0].astype(BF16)
    o_q = 4 * D
    o_lr = o_q + 2 * DK + 2 * D
    o_m = o_lr + 2 * RANK
    wlr = jnp.pad(w[:, o_lr:o_m], ((0, 0), (0, 128 - 2 * RANK)))
    wbz = w[:, o_q + 2 * DK + D:o_lr]
    wma = w[:, o_m:o_m + D]
    wmb = w[:, o_m + D:]
    wg = jnp.zeros((128, 2 * DK), F32)
    wg = wg.at[:RANK, :DK].set(w_gate_f[0]).at[RANK:2 * RANK, DK:].set(w_gate_b[0]).astype(BF16)
    bg = jnp.concatenate([b_gate_f[0], b_gate_b[0]])[None, :]
    ng = norm_g[0][None, :]
    cw = conv_w[0]
    cb = conv_b[0][None, :]
    gln = gla_norm_g[0][None, :]
    wao = w_a_out[0].astype(BF16)
    wbo = w_b_out[0].astype(BF16)
    wo = w_out[0].astype(BF16)
    fng = final_norm_g[None, :]

    nb = x_prompt.shape[0]
    mod = _modulation(jnp.concatenate([c_prompt, c_sample], axis=0), w_ada[0], b_ada[0][None, :])
    mod3 = mod.reshape(mod.shape[0], 3, D)

    def encode(x, m3):
        pre = _pre(x, m3, ng, w, wlr, wma, cw, cb, wg, bg, wao)
        return _post(x, *pre, m3, gln, wbz, wmb, wbo, wo, fng)

    return (encode(x_prompt, mod3[:nb]), encode(x_sample, mod3[nb:]))
```

```python
import functools

import jax
import jax.numpy as jnp
from jax import lax
from jax.experimental import pallas as pl
from jax.experimental.pallas import tpu as pltpu

D = 1024
HEADS = 4
DK = 512
HK = DK // HEADS
HV = D // HEADS
RANK = 16
GATE_TEMP = 16.0
EPS = 1e-6
CHUNK = 128
TILE = 512
HALO = 16
SLAB = 256
LOG2E = 1.4426950408889634
EXP2_CLAMP = 115.0
VMEM_LIMIT = 58 * 1024 * 1024

F32 = jnp.float32
BF16 = jnp.bfloat16


def _dot(a, b):
    return jnp.dot(a, b, preferred_element_type=F32)


def _sigmoid(x):
    return 0.5 + 0.5 * jnp.tanh(0.5 * x)


def _silu(x):
    h = 0.5 * x
    return h + h * jnp.tanh(h)


def _rms_scale(x):
    return lax.rsqrt(jnp.mean(x * x, axis=-1, keepdims=True) + EPS)


def _const_spec(shape):
    zeros = (0,) * len(shape)
    return pl.BlockSpec(shape, lambda b, i: zeros, pipeline_mode=pl.Buffered(1))


def _mod_kernel(c_ref, w_ref, b_ref, o_ref):
    o_ref[...] = jnp.dot(_silu(c_ref[...]), w_ref[...], preferred_element_type=F32,
                         precision=lax.Precision.HIGHEST) + b_ref[...]


def _modulation(c, w_ada, b_ada):
    n = c.shape[0]
    return pl.pallas_call(
        _mod_kernel,
        out_shape=jax.ShapeDtypeStruct((n, 3 * D), F32),
        compiler_params=pltpu.CompilerParams(vmem_limit_bytes=VMEM_LIMIT),
        name="adaln_mod",
    )(c, w_ada, b_ada)


def _pre_kernel(x_ref, xp_ref, xn_ref, mod_ref, ng_ref, wab_ref, wac_ref, wax_ref, waz_ref,
                wqk_ref, wv_ref, wlr_ref, wma_ref,
                cw_ref, cb_ref, wg_ref, bg_ref, wao_ref,
                qfb_ref, kft_ref, kbt_ref, v_ref, hb_ref, yam_ref,
                sbs_ref, uft_ref, cvec_ref, state_ref, ga_ref, *, n_tiles):
    i = pl.program_id(1)
    tile = n_tiles - 1 - i
    T = x_ref.shape[1]

    @pl.when(i == 0)
    def _():
        state_ref[...] = jnp.zeros_like(state_ref)

    shift = mod_ref[0, 0:1, :]
    gain = ng_ref[...] * (1.0 + mod_ref[0, 1:2, :])

    def normed(xf):
        return (xf * _rms_scale(xf) * gain + shift).astype(BF16)

    hb = normed(x_ref[0])
    hb_ref[0] = hb
    zero_halo = jnp.zeros((HALO, D), BF16)
    hprev = jnp.where(tile > 0, normed(xp_ref[0]), zero_halo)
    hnext = jnp.where(tile < n_tiles - 1, normed(xn_ref[0]), zero_halo)
    hx = jnp.concatenate([hprev, hb, hnext], axis=0)

    lr = _dot(hb, wlr_ref[...]).astype(BF16)
    qk = _dot(hb, wqk_ref[...])
    z = _dot(lr, wg_ref[...]) + bg_ref[...]
    g2 = (jnp.minimum(z, 0.0) - jnp.log1p(jnp.exp(-jnp.abs(z)))) * (LOG2E / GATE_TEMP)

    q = qk[:, :DK]
    k = qk[:, DK:2 * DK]
    vb = _dot(hb, wv_ref[...]).astype(BF16)
    v_ref[0] = vb
    vt = jnp.transpose(vb)

    r = lax.broadcasted_iota(jnp.int32, (CHUNK, 2 * CHUNK), 0)
    c = lax.broadcasted_iota(jnp.int32, (CHUNK, 2 * CHUNK), 1) % CHUNK
    tri_incl = (c <= r).astype(BF16)
    tri_strict = (c < r).astype(BF16)
    mid = CHUNK // 2
    hilo_of_chunk = {}
    staged = {}

    def pair_cols(hp):
        return slice(hp * 2 * HK, (hp + 1) * 2 * HK), slice(DK + hp * 2 * HK, DK + (hp + 1) * 2 * HK)

    def gla_cumsum(ci, hp):
        rows = slice(ci * CHUNK, (ci + 1) * CHUNK)
        if ci not in hilo_of_chunk:
            gc = g2[rows]
            hi = gc.astype(BF16)
            lo = (gc - hi.astype(F32)).astype(BF16)
            hilo_of_chunk[ci] = (jnp.concatenate([hi, lo], axis=0), gc)
        hilo, _ = hilo_of_chunk[ci]
        pc, pcb = pair_cols(hp)
        staged[ci, hp, 'cum'] = (_dot(tri_incl, hilo[:, pc]), _dot(tri_strict, hilo[:, pcb]))

    def gla_operands(ci, hp):
        rows = slice(ci * CHUNK, (ci + 1) * CHUNK)
        pc, pcb = pair_cols(hp)
        gc = hilo_of_chunk[ci][1]
        gf, eb = staged.pop((ci, hp, 'cum'))
        qc, kc = q[rows, pc], k[rows, pc]
        gm = gf[mid - 1:mid, :]
        gl = gf[CHUNK - 1:CHUNK, :]
        ef = jnp.exp2(lax.clamp(-EXP2_CLAMP, gf - gm, EXP2_CLAMP))
        qaf = (qc * ef).astype(BF16)
        kaf = kc * (1.0 / ef)
        kft_ref[0, ci, pc, :] = jnp.transpose(kaf.astype(BF16))
        em = eb[mid:mid + 1, :]
        et = eb[CHUNK - 1:CHUNK, :] + gc[CHUNK - 1:CHUNK, pcb]
        ebx = jnp.exp2(lax.clamp(-EXP2_CLAMP, em - eb, EXP2_CLAMP))
        qab = (qc * ebx).astype(BF16)
        kab = kc * (1.0 / ebx)
        kbt_ref[0, ci, pc, :] = jnp.transpose(kab.astype(BF16))
        cvec_ref[0, ci, :, pc] = jnp.concatenate(
            [jnp.exp2(gl), jnp.exp2(gm), jnp.zeros((6, 2 * HK), F32)], axis=0)
        for hh in range(2):
            h = 2 * hp + hh
            lc = slice(hh * HK, (hh + 1) * HK)
            qfb_ref[0, rows, 2 * h * HK:(2 * h + 1) * HK] = qaf[:, lc]
            qfb_ref[0, rows, (2 * h + 1) * HK:(2 * h + 2) * HK] = qab[:, lc]
        staged[ci, hp, 'ks'] = ((kaf * jnp.exp2(gl - gm)).astype(BF16), (kab * jnp.exp2(em)).astype(BF16),
                                jnp.exp2(et - em), jnp.exp2(et))

    def gla_state(ci, hp):
        rows = slice(ci * CHUNK, (ci + 1) * CHUNK)
        ksf, ksb, mu_b, gamma_b = staged.pop((ci, hp, 'ks'))
        for hh in range(2):
            h = 2 * hp + hh
            lc = slice(hh * HK, (hh + 1) * HK)
            vc = slice(h * HV, (h + 1) * HV)
            u = _dot(vt[vc, rows], jnp.concatenate([ksf[:, lc], ksb[:, lc]], axis=1))
            uft_ref[0, ci, vc, :] = u[:, :HK].astype(BF16)
            st = state_ref[h]
            sbs_ref[0, ci, h * HK:(h + 1) * HK, :] = jnp.transpose(st * mu_b[:, lc]).astype(BF16)
            state_ref[h] = st * gamma_b[:, lc] + u[:, HK:]

    inner = slice(HALO, T + HALO)
    slab_vals = {}

    def conv_piece(s, part):
        cols = slice(s * SLAB, (s + 1) * SLAB)
        if part == 0:
            slab_vals[s, 'c'] = _dot(hx, wac_ref[:, cols])
        elif part == 1:
            p = slab_vals.pop((s, 'c')) * _dot(hx, wax_ref[:, cols])
            p_prev = pltpu.roll(p, 1, axis=0)[inner]
            p_next = pltpu.roll(p, T + 2 * HALO - 1, axis=0)[inner]
            slab_vals[s, 'u'] = (cw_ref[0:1, cols] * p_prev + cw_ref[1:2, cols] * p[inner]
                                 + cw_ref[2:3, cols] * p_next + cb_ref[:, cols])
        elif part == 2:
            slab_vals[s, 'bu'] = _dot(hb, wab_ref[:, cols]) * slab_vals.pop((s, 'u'))
        else:
            az = _dot(hb, waz_ref[:, cols])
            ga_ref[:, cols] = (slab_vals.pop((s, 'bu')) * _silu(az)).astype(BF16)

    gla_pieces = [(ci, hp) for ci in reversed(range(T // CHUNK)) for hp in range(HEADS // 2)]
    conv_pieces = [(s, part) for s in range(D // SLAB) for part in range(4)]
    assert len(conv_pieces) == 2 * len(gla_pieces)
    gla_cumsum(*gla_pieces[0])
    for j, gp in enumerate(gla_pieces):
        conv_piece(*conv_pieces[2 * j])
        gla_operands(*gp)
        if j + 1 < len(gla_pieces):
            gla_cumsum(*gla_pieces[j + 1])
        conv_piece(*conv_pieces[2 * j + 1])
        gla_state(*gp)

    for hc in range(2):
        cols = slice(hc * D // 2, (hc + 1) * D // 2)
        m_a = _dot(hb, wma_ref[:, cols])
        y_a = _dot(ga_ref[...], wao_ref[:, cols])
        yam_ref[0, :, cols] = (_sigmoid(m_a) * y_a).astype(BF16)


def _pre(x, mod3, ng, w, wlr, wma, cw, cb, wg, bg, wao):
    B, L, _ = x.shape
    T = TILE
    nt = L // T
    nc = T // CHUNK
    tok = lambda w: pl.BlockSpec((1, T, w), lambda b, i: (b, nt - 1 - i, 0))
    hpt = T // HALO
    wcol = lambda j: pl.BlockSpec((D, D), lambda b, i: (0, j), pipeline_mode=pl.Buffered(1))
    in_specs = [
        tok(D),
        pl.BlockSpec((1, HALO, D), lambda b, i: (b, jnp.maximum((nt - 1 - i) * hpt - 1, 0), 0)),
        pl.BlockSpec((1, HALO, D), lambda b, i: (b, jnp.minimum((nt - i) * hpt, L // HALO - 1), 0)),
        pl.BlockSpec((1, 3, D), lambda b, i: (b, 0, 0)),
        _const_spec((1, D)),
        wcol(0), wcol(1), wcol(2), wcol(3),
        wcol(4), wcol(5),
        _const_spec((D, 128)),
        _const_spec((D, D)),
        _const_spec((3, D)),
        _const_spec((1, D)),
        _const_spec((128, 2 * DK)),
        _const_spec((1, 2 * DK)),
        _const_spec((D, D)),
    ]
    out_shape = [
        jax.ShapeDtypeStruct((B, L, 2 * DK), BF16),
        jax.ShapeDtypeStruct((B, L // CHUNK, DK, CHUNK), BF16),
        jax.ShapeDtypeStruct((B, L // CHUNK, DK, CHUNK), BF16),
        jax.ShapeDtypeStruct((B, L, D), BF16),
        jax.ShapeDtypeStruct((B, L, D), BF16),
        jax.ShapeDtypeStruct((B, L, D), BF16),
        jax.ShapeDtypeStruct((B, L // CHUNK, DK, HV), BF16),
        jax.ShapeDtypeStruct((B, L // CHUNK, HEADS * HV, HK), BF16),
        jax.ShapeDtypeStruct((B, L // CHUNK, 8, DK), F32),
    ]
    st_spec = pl.BlockSpec((1, nc, HEADS * HV, HK), lambda b, i: (b, nt - 1 - i, 0, 0))
    kt_spec = pl.BlockSpec((1, nc, DK, CHUNK), lambda b, i: (b, nt - 1 - i, 0, 0))
    sb_spec = pl.BlockSpec((1, nc, DK, HV), lambda b, i: (b, nt - 1 - i, 0, 0))
    out_specs = [tok(2 * DK), kt_spec, kt_spec, tok(D), tok(D), tok(D), sb_spec, st_spec,
                 pl.BlockSpec((1, nc, 8, DK), lambda b, i: (b, nt - 1 - i, 0, 0))]
    return pl.pallas_call(
        functools.partial(_pre_kernel, n_tiles=nt),
        grid=(B, nt),
        in_specs=in_specs,
        out_specs=out_specs,
        out_shape=out_shape,
        scratch_shapes=[pltpu.VMEM((HEADS, HV, HK), F32), pltpu.VMEM((T, D), BF16)],
        compiler_params=pltpu.CompilerParams(
            dimension_semantics=("arbitrary", "arbitrary"), vmem_limit_bytes=VMEM_LIMIT),
        name="encoder_pre",
    )(x, x, x, mod3, ng, w, w, w, w, w, w, wlr, wma, cw, cb, wg, bg, wao)


def _post_kernel(x_ref, qfb_ref, kft_ref, kbt_ref, v_ref, hb_ref, yam_ref,
                 sbt_ref, uft_ref, cvec_ref, mod_ref, gln_ref, wbz_ref, wmb_ref, wbo_ref, wo_ref, fng_ref,
                 o_ref, state_ref, yin_ref):
    i = pl.program_id(1)
    T = x_ref.shape[1]
    n_chunks = T // CHUNK

    @pl.when(i == 0)
    def _():
        state_ref[...] = jnp.zeros_like(state_ref)

    r = lax.broadcasted_iota(jnp.int32, (CHUNK, CHUNK), 0)
    c = lax.broadcasted_iota(jnp.int32, (CHUNK, CHUNK), 1)
    causal = c <= r
    zeros = jnp.zeros((CHUNK, HK), BF16)

    att = {}
    for ci in range(n_chunks):
        rows = slice(ci * CHUNK, (ci + 1) * CHUNK)
        for h in range(HEADS):
            kc = slice(h * HK, (h + 1) * HK)
            kblk = jnp.concatenate(
                [jnp.concatenate([kft_ref[0, ci, kc, :], zeros], axis=1),
                 jnp.concatenate([zeros, kbt_ref[0, ci, kc, :]], axis=1)], axis=0)
            a2 = _dot(qfb_ref[0, rows, 2 * h * HK:(2 * h + 2) * HK], kblk)
            att[ci, h] = jnp.where(causal, a2[:, :CHUNK], a2[:, CHUNK:]).astype(BF16)

    sb = _silu(_dot(hb_ref[0], wbz_ref[...]))

    scat = {}
    for h in range(HEADS):
        kc = slice(h * HK, (h + 1) * HK)
        vc = slice(h * HV, (h + 1) * HV)
        st = state_ref[h]
        for ci in range(n_chunks):
            cv = cvec_ref[0, ci]
            scat[ci, h] = jnp.concatenate(
                [jnp.transpose(st * cv[1:2, kc]).astype(BF16), sbt_ref[0, ci, kc, :]], axis=0)
            st = st * cv[0:1, kc] + uft_ref[0, ci, vc, :].astype(F32)
        state_ref[h] = st

    for ci in range(n_chunks):
        rows = slice(ci * CHUNK, (ci + 1) * CHUNK)
        for h in range(HEADS):
            vc = slice(h * HV, (h + 1) * HV)
            o = (_dot(att[ci, h], v_ref[0, rows, vc])
                 + _dot(qfb_ref[0, rows, 2 * h * HK:(2 * h + 2) * HK], scat[ci, h]))
            o = o * lax.rsqrt(jnp.mean(o * o, axis=-1, keepdims=True) + EPS * HK) * gln_ref[:, vc]
            yin_ref[rows, vc] = (o * sb[rows, vc]).astype(BF16)

    smb = _sigmoid(_dot(hb_ref[0], wmb_ref[...]))
    half = T // 2
    merged = {}
    for hf in range(2):
        rows = slice(hf * half, (hf + 1) * half)
        y_b = _dot(yin_ref[rows, :], wbo_ref[...])
        merged[hf] = (yam_ref[0, rows, :].astype(F32) + smb[rows] * y_b).astype(BF16)
    for hf in range(2):
        rows = slice(hf * half, (hf + 1) * half)
        y = x_ref[0, rows, :] + mod_ref[0, 2:3, :] * _dot(merged[hf], wo_ref[...])
        o_ref[0, rows, :] = y * _rms_scale(y) * fng_ref[...]


def _post(x, qfb, kft, kbt, v, hb, yam, sbt, uft, cvec, mod3, gln, wbz, wmb, wbo, wo, fng):
    B, L, _ = x.shape
    T = TILE
    nt = L // T
    nc = T // CHUNK
    tok = lambda w: pl.BlockSpec((1, T, w), lambda b, i: (b, i, 0))
    st_spec = pl.BlockSpec((1, nc, HEADS * HV, HK), lambda b, i: (b, i, 0, 0))
    kt_spec = pl.BlockSpec((1, nc, DK, CHUNK), lambda b, i: (b, i, 0, 0))
    sb_spec = pl.BlockSpec((1, nc, DK, HV), lambda b, i: (b, i, 0, 0))
    in_specs = [tok(D), tok(2 * DK), kt_spec, kt_spec, tok(D), tok(D), tok(D),
                sb_spec, st_spec,
                pl.BlockSpec((1, nc, 8, DK), lambda b, i: (b, i, 0, 0)),
                pl.BlockSpec((1, 3, D), lambda b, i: (b, 0, 0)),
                _const_spec((1, D)), _const_spec((D, D)), _const_spec((D, D)), _const_spec((D, D)),
                _const_spec((D, D)), _const_spec((1, D))]
    return pl.pallas_call(
        _post_kernel,
        grid=(B, nt),
        in_specs=in_specs,
        out_specs=tok(D),
        out_shape=jax.ShapeDtypeStruct((B, L, D), F32),
        scratch_shapes=[pltpu.VMEM((HEADS, HV, HK), F32), pltpu.VMEM((T, D), BF16)],
        compiler_params=pltpu.CompilerParams(
            dimension_semantics=("arbitrary", "arbitrary"), vmem_limit_bytes=VMEM_LIMIT),
        name="encoder_post",
    )(x, qfb, kft, kbt, v, hb, yam, sbt, uft, cvec, mod3, gln, wbz, wmb, wbo, wo, fng)


def kernel(x_prompt, x_sample, c_prompt, c_sample, w_ada, b_ada, norm_g, w_in, conv_w, conv_b,
           w_gate_f, b_gate_f, w_gate_b, b_gate_b, gla_norm_g, w_a_out, w_b_out, w_out,
           final_norm_g):
    assert w_ada.shape[0] == 1, "one layer"
    w = w_in[0].astype(BF16)
    o_q = 4 * D
    o_lr = o_q + 2 * DK + 2 * D
    o_m = o_lr + 2 * RANK
    wlr = jnp.pad(w[:, o_lr:o_m], ((0, 0), (0, 128 - 2 * RANK)))
    wbz = w[:, o_q + 2 * DK + D:o_lr]
    wma = w[:, o_m:o_m + D]
    wmb = w[:, o_m + D:]
    wg = jnp.zeros((128, 2 * DK), F32)
    wg = wg.at[:RANK, :DK].set(w_gate_f[0]).at[RANK:2 * RANK, DK:].set(w_gate_b[0]).astype(BF16)
    bg = jnp.concatenate([b_gate_f[0], b_gate_b[0]])[None, :]
    ng = norm_g[0][None, :]
    cw = conv_w[0]
    cb = conv_b[0][None, :]
    gln = gla_norm_g[0][None, :]
    wao = w_a_out[0].astype(BF16)
    wbo = w_b_out[0].astype(BF16)
    wo = w_out[0].astype(BF16)
    fng = final_norm_g[None, :]

    nb = x_prompt.shape[0]
    mod = _modulation(jnp.concatenate([c_prompt, c_sample], axis=0), w_ada[0], b_ada[0][None, :])
    mod3 = mod.reshape(mod.shape[0], 3, D)

    def encode(x, m3):
        pre = _pre(x, m3, ng, w, wlr, wma, cw, cb, wg, bg, wao)
        return _post(x, *pre, m3, gln, wbz, wmb, wbo, wo, fng)

    return (encode(x_prompt, mod3[:nb]), encode(x_sample, mod3[nb:]))
```
